```python
import jax, jax.numpy as jnp
from jax import lax

D_MODEL = 4096
BATCH = 1
SEQ = 8192
DEPTH = 1

HEAD_DIM = 128
ROPE_THETA = 10000.0
NORM_EPS = 1e-6
NEG_INF = -1e30
QBLK = 128

DIL_WINDOWS = (128, 512, 2048)
DIL_RATES = (1, 4, 16)
N_DIL = 3
DIL_HEADS = 4
DIL_NKEYS = DIL_WINDOWS[0] // DIL_RATES[0] + 1
A_HEADS = N_DIL * DIL_HEADS
A_W = A_HEADS * HEAD_DIM
A_OUT_W = DIL_HEADS * HEAD_DIM

NSA_HEADS = 16
NSA_KV = 4
NSA_QPG = NSA_HEADS // NSA_KV
B_W = NSA_HEADS * HEAD_DIM
KV_W = NSA_KV * HEAD_DIM
CMP_LEN = 32
CMP_STRIDE = 16
CMP_HID = HEAD_DIM
SEL_LEN = 64
SEL_TOP = 16
WIN = 512
FORCE_BONUS = 1000.0
N_NSA_GATES = 3

D_FF = -(-8 * D_MODEL // (3 * 256)) * 256
PLE_DIM = 256

IN_WIDTHS = (A_W, A_W, A_W, B_W, KV_W, KV_W, KV_W, KV_W, KV_W, KV_W, NSA_HEADS * N_NSA_GATES, 2 * D_MODEL)
IN_W = sum(IN_WIDTHS)

kernel_name = "hybrid_dilated_nsa_gated_block"


def rms_norm(x, g):
    xf = x.astype(jnp.float32)
    y = xf * lax.rsqrt(jnp.mean(xf * xf, axis=-1, keepdims=True) + NORM_EPS)
    return (y * g.astype(jnp.float32)).astype(x.dtype)


def rope(x, pos):
    half = HEAD_DIM // 2
    inv = ROPE_THETA ** (-jnp.arange(half, dtype=jnp.float32) / half)
    ang = pos.astype(jnp.float32)[:, None] * inv[None, :]
    shape = (pos.shape[0],) + (1,) * (x.ndim - 3) + (half,)
    cos = jnp.cos(ang).reshape(shape)
    sin = jnp.sin(ang).reshape(shape)
    xf = x.astype(jnp.float32)
    x1, x2 = xf[..., :half], xf[..., half:]
    return jnp.concatenate([x1 * cos - x2 * sin, x2 * cos + x1 * sin], axis=-1).astype(x.dtype)


def masked_softmax(logits, mask):
    return jax.nn.softmax(jnp.where(mask, logits.astype(jnp.float32), NEG_INF), axis=-1)


def dilated_attention(q, k, v):
    B, S = q.shape[0], q.shape[1]
    scale = HEAD_DIM ** -0.5
    rates = jnp.array(DIL_RATES, dtype=jnp.int32)
    offs = rates[:, None] * jnp.arange(DIL_NKEYS, dtype=jnp.int32)[None, :]
    g_ix = jnp.arange(N_DIL)[None, :, None]

    def block(c):
        start = c * QBLK
        pos = start + jnp.arange(QBLK)
        qb = lax.dynamic_slice_in_dim(q, start, QBLK, axis=1)
        kp = pos[:, None, None] - offs[None]
        valid = kp >= 0
        kpc = jnp.maximum(kp, 0)
        kb = k[:, kpc, g_ix]
        vb = v[:, kpc, g_ix]
        logit = jnp.einsum('bqghd,bqgkhd->bqghk', qb, kb).astype(jnp.float32) * scale
        logit = jnp.where(valid[None, :, :, None, :], logit, NEG_INF)
        m = jnp.max(logit, axis=-1, keepdims=True)
        e = jnp.exp(logit - m)
        den = jnp.sum(e, axis=-1, keepdims=True)
        o = jnp.einsum('bqghk,bqgkhd->bqghd', (e / den).astype(v.dtype), vb)
        log_den = (m + jnp.log(den))[..., 0]
        wgt = jax.nn.softmax(log_den, axis=2)
        return jnp.einsum('bqgh,bqghd->bqhd', wgt.astype(o.dtype), o)

    out = lax.map(block, jnp.arange(S // QBLK))
    return jnp.moveaxis(out, 0, 1).reshape(B, S, A_OUT_W)


def compress(kv, pe, w1, w2):
    B, S = kv.shape[0], kv.shape[1]
    nc = (S - CMP_LEN) // CMP_STRIDE + 1
    idx = jnp.arange(nc)[:, None] * CMP_STRIDE + jnp.arange(CMP_LEN)[None, :]
    blk = kv[:, idx] + pe[None, None, :, None, :]
    blk = blk.transpose(0, 1, 3, 2, 4).reshape(B, nc, NSA_KV, CMP_LEN * HEAD_DIM)
    return jax.nn.gelu(blk @ w1) @ w2


def overlap_matrix(nc, ns):
    cs = jnp.arange(nc)[:, None] * CMP_STRIDE
    ss = jnp.arange(ns)[None, :] * SEL_LEN
    ov = jnp.clip(jnp.minimum(cs + CMP_LEN, ss + SEL_LEN) - jnp.maximum(cs, ss), 0, None)
    return ov.astype(jnp.float32) / CMP_LEN


def native_sparse_attention(q_raw, q_rot, k_c, v_c, k_s, v_s, k_w, v_w, gates):
    B, S = q_raw.shape[0], q_raw.shape[1]
    nc = k_c.shape[1]
    ns = S // SEL_LEN
    n_top = min(SEL_TOP, ns)
    scale = HEAD_DIM ** -0.5
    ov = overlap_matrix(nc, ns)
    cmp_end = jnp.arange(nc) * CMP_STRIDE + CMP_LEN - 1
    sblk = jnp.arange(ns)
    ks_blk = k_s.reshape(B, ns, SEL_LEN, NSA_KV, HEAD_DIM).transpose(0, 3, 1, 2, 4)
    vs_blk = v_s.reshape(B, ns, SEL_LEN, NSA_KV, HEAD_DIM).transpose(0, 3, 1, 2, 4)
    pad = ((0, 0), (WIN, 0), (0, 0), (0, 0))
    k_wp = jnp.pad(k_w, pad)
    v_wp = jnp.pad(v_w, pad)
    b_ix = jnp.arange(B)[:, None, None, None]
    g_ix = jnp.arange(NSA_KV)[None, None, :, None]

    def block(c):
        start = c * QBLK
        pos = start + jnp.arange(QBLK)
        qr = lax.dynamic_slice_in_dim(q_raw, start, QBLK, axis=1)
        qo = lax.dynamic_slice_in_dim(q_rot, start, QBLK, axis=1)
        gb = lax.dynamic_slice_in_dim(gates, start, QBLK, axis=1)
        cl = jnp.einsum('bqgjd,bngd->bqgjn', qr, k_c) * scale
        cvalid = cmp_end[None, :] <= pos[:, None]
        has_any = (pos >= CMP_LEN - 1).astype(jnp.float32)
        p_c = masked_softmax(cl, cvalid[None, :, None, None, :]) * has_any[None, :, None, None, None]
        o_c = jnp.einsum('bqgjn,bngd->bqgjd', p_c.astype(v_c.dtype), v_c)
        imp = jnp.einsum('bqgjn,nm->bqgm', p_c, ov)
        cur = pos // SEL_LEN
        forced = ((sblk[None] == 0) | (sblk[None] == cur[:, None]) | (sblk[None] == cur[:, None] - 1)).astype(jnp.float32)
        svalid = sblk[None] * SEL_LEN <= pos[:, None]
        score = jnp.where(svalid[None, :, None, :], imp + FORCE_BONUS * forced[None, :, None, :], NEG_INF)
        _, sel = lax.top_k(score, n_top)
        ksb = ks_blk[b_ix, g_ix, sel]
        vsb = vs_blk[b_ix, g_ix, sel]
        kpos = sel[..., None] * SEL_LEN + jnp.arange(SEL_LEN)
        smask = (kpos <= pos[None, :, None, None, None]).reshape(B, QBLK, NSA_KV, 1, n_top * SEL_LEN)
        sl = jnp.einsum('bqgjd,bqgkld->bqgjkl', qo, ksb) * scale
        sl = sl.reshape(B, QBLK, NSA_KV, NSA_QPG, n_top * SEL_LEN)
        p_s = masked_softmax(sl, smask).reshape(B, QBLK, NSA_KV, NSA_QPG, n_top, SEL_LEN)
        o_s = jnp.einsum('bqgjkl,bqgkld->bqgjd', p_s.astype(vsb.dtype), vsb)
        kwb = lax.dynamic_slice_in_dim(k_wp, start, QBLK + WIN, axis=1)
        vwb = lax.dynamic_slice_in_dim(v_wp, start, QBLK + WIN, axis=1)
        wpos = start - WIN + jnp.arange(QBLK + WIN)
        wmask = (wpos[None] <= pos[:, None]) & (pos[:, None] - wpos[None] < WIN) & (wpos[None] >= 0)
        wl = jnp.einsum('bqgjd,bkgd->bqgjk', qo, kwb) * scale
        p_w = masked_softmax(wl, wmask[None, :, None, None, :])
        o_w = jnp.einsum('bqgjk,bkgd->bqgjd', p_w.astype(vwb.dtype), vwb)
        return gb[..., 0:1] * o_c + gb[..., 1:2] * o_s + gb[..., 2:3] * o_w

    out = lax.map(block, jnp.arange(S // QBLK))
    return jnp.moveaxis(out, 0, 1).reshape(B, S, B_W)


def setup_inputs(seed: int = 0) -> dict:
    key = jax.random.key(seed)
    ks = jax.random.split(key, 22)
    f32 = jnp.float32

    def w(k, shape, fan_in):
        return jax.random.normal(k, shape, f32) * fan_in ** -0.5

    def gain(k):
        return 1.0 + 0.1 * jax.random.normal(k, (DEPTH, D_MODEL), f32)

    return {
        "x": jax.random.normal(ks[0], (BATCH, SEQ, D_MODEL), f32),
        "p": jax.random.normal(ks[1], (DEPTH, BATCH, SEQ, PLE_DIM), f32),
        "g_mix_pre": gain(ks[2]),
        "w_in": w(ks[3], (DEPTH, D_MODEL, IN_W), D_MODEL),
        "pe_ck": 0.5 * jax.random.normal(ks[4], (DEPTH, CMP_LEN, HEAD_DIM), f32),
        "w_ck1": w(ks[5], (DEPTH, CMP_LEN * HEAD_DIM, CMP_HID), CMP_LEN * HEAD_DIM),
        "w_ck2": w(ks[6], (DEPTH, CMP_HID, HEAD_DIM), CMP_HID),
        "pe_cv": 0.5 * jax.random.normal(ks[7], (DEPTH, CMP_LEN, HEAD_DIM), f32),
        "w_cv1": w(ks[8], (DEPTH, CMP_LEN * HEAD_DIM, CMP_HID), CMP_LEN * HEAD_DIM),
        "w_cv2": w(ks[9], (DEPTH, CMP_HID, HEAD_DIM), CMP_HID),
        "w_a": w(ks[10], (DEPTH, A_OUT_W, D_MODEL), A_OUT_W),
        "w_b": w(ks[11], (DEPTH, B_W, D_MODEL), B_W),
        "w_out": w(ks[12], (DEPTH, D_MODEL, D_MODEL), D_MODEL),
        "g_mix_post": gain(ks[13]),
        "g_ffn_pre": gain(ks[14]),
        "w_gu": w(ks[15], (DEPTH, D_MODEL, 2 * D_FF), D_MODEL),
        "w_down": w(ks[16], (DEPTH, D_FF, D_MODEL), D_FF),
        "g_ffn_post": gain(ks[17]),
        "g_ple_pre": gain(ks[18]),
        "w_ple_gate": w(ks[19], (DEPTH, D_MODEL, D_MODEL), D_MODEL),
        "w_ple": w(ks[20], (DEPTH, PLE_DIM, D_MODEL), PLE_DIM),
        "g_ple_post": gain(ks[21]),
    }


def reference(x, p, g_mix_pre, w_in, pe_ck, w_ck1, w_ck2, pe_cv, w_cv1, w_cv2, w_a, w_b, w_out,
              g_mix_post, g_ffn_pre, w_gu, w_down, g_ffn_post, g_ple_pre, w_ple_gate, w_ple, g_ple_post):
    B, S, D = x.shape
    pos = jnp.arange(S)
    splits = []
    acc = 0
    for wd in IN_WIDTHS[:-1]:
        acc += wd
        splits.append(acc)
    for i in range(DEPTH):
        h = rms_norm(x, g_mix_pre[i])
        proj = h @ w_in[i]
        (qa, ka, va, qb, kc, vc, ksl, vsl, kwn, vwn, g_nsa, g_mix) = jnp.split(proj, splits, axis=-1)
        a_shape = (B, S, N_DIL, DIL_HEADS, HEAD_DIM)
        y_a = dilated_attention(rope(qa.reshape(a_shape), pos), rope(ka.reshape(a_shape), pos),
                                va.reshape(a_shape)) @ w_a[i]
        kv_shape = (B, S, NSA_KV, HEAD_DIM)
        q_raw = qb.reshape(B, S, NSA_KV, NSA_QPG, HEAD_DIM)
        k_c = compress(kc.reshape(kv_shape), pe_ck[i], w_ck1[i], w_ck2[i])
        v_c = compress(vc.reshape(kv_shape), pe_cv[i], w_cv1[i], w_cv2[i])
        gates = jax.nn.sigmoid(g_nsa.reshape(B, S, NSA_KV, NSA_QPG, N_NSA_GATES))
        y_b = native_sparse_attention(q_raw, rope(q_raw, pos), k_c, v_c,
                                      rope(ksl.reshape(kv_shape), pos), vsl.reshape(kv_shape),
                                      rope(kwn.reshape(kv_shape), pos), vwn.reshape(kv_shape),
                                      gates) @ w_b[i]
        gate_a, gate_b = jnp.split(jax.nn.sigmoid(g_mix), 2, axis=-1)
        mixed = (gate_a * y_a + gate_b * y_b) @ w_out[i]
        x = x + rms_norm(mixed, g_mix_post[i])
        h = rms_norm(x, g_ffn_pre[i])
        gt, up = jnp.split(h @ w_gu[i], 2, axis=-1)
        x = x + rms_norm((jax.nn.silu(gt) * up) @ w_down[i], g_ffn_post[i])
        ple = (p[i] @ w_ple[i]) * jax.nn.sigmoid(rms_norm(x, g_ple_pre[i]) @ w_ple_gate[i])
        x = x + rms_norm(ple, g_ple_post[i])
    return x
```

```python
import functools

import jax
import jax.numpy as jnp
from jax import lax
from jax.experimental import pallas as pl
from jax.experimental.pallas import tpu as pltpu

F32 = jnp.float32
BF16 = jnp.bfloat16

HEAD_DIM = 128
LANES = 128
ROPE_THETA = 10000.0
NORM_EPS = 1e-6
NEG_INF = -1e30
SCALE = HEAD_DIM ** -0.5

DIL_RATES = (1, 4, 16)
DIL_HEADS = 4
DIL_SPAN = 128
DIL_W = DIL_HEADS * HEAD_DIM
N_DIL = len(DIL_RATES)
A_W = N_DIL * DIL_W

NSA_HEADS = 16
NSA_KV = 4
NSA_QPG = NSA_HEADS // NSA_KV
B_W = NSA_HEADS * HEAD_DIM
KV_W = NSA_KV * HEAD_DIM
CMP_LEN = 32
CMP_STRIDE = 16
SEL_LEN = 64
SEL_TOP = 16
WIN = 512
FORCE_BONUS = 1000.0
N_NSA_GATES = 3

OFF_QA, OFF_KA, OFF_VA = 0, A_W, 2 * A_W
OFF_QB = 3 * A_W
OFF_KC = OFF_QB + B_W
OFF_VC = OFF_KC + KV_W
OFF_KS = OFF_VC + KV_W
OFF_GN = OFF_KS + 4 * KV_W
OFF_GM = OFF_GN + NSA_HEADS * N_NSA_GATES

VMEM_LIMIT_BYTES = 56 * 1024 * 1024


def _params(*sem):
    return pltpu.CompilerParams(dimension_semantics=sem, vmem_limit_bytes=VMEM_LIMIT_BYTES)


def _rms(v, g):
    return v * lax.rsqrt(jnp.mean(v * v, axis=-1, keepdims=True) + NORM_EPS) * g


def _norm_kernel(x_ref, g_ref, h_ref):
    h_ref[...] = _rms(x_ref[...], g_ref[...]).astype(h_ref.dtype)


def _resid_norm_kernel(x_ref, y_ref, gp_ref, gn_ref, xo_ref, h_ref):
    xn = x_ref[...] + _rms(y_ref[...], gp_ref[...])
    xo_ref[...] = xn
    h_ref[...] = _rms(xn, gn_ref[...]).astype(h_ref.dtype)


def _resid_kernel(x_ref, y_ref, gp_ref, xo_ref):
    xo_ref[...] = x_ref[...] + _rms(y_ref[...], gp_ref[...])


def _row_block(s):
    return min(256, s)


def rmsnorm_bf16(x, g):
    s, d = x.shape
    tm = _row_block(s)
    row = pl.BlockSpec((tm, d), lambda i: (i, 0))
    vec = pl.BlockSpec((1, d), lambda i: (0, 0))
    return pl.pallas_call(
        _norm_kernel, grid=(s // tm,), in_specs=[row, vec], out_specs=row,
        out_shape=jax.ShapeDtypeStruct((s, d), BF16), compiler_params=_params("parallel"),
        name="rmsnorm")(x, g.reshape(1, d))


def resid_norm(x, y, g_post, g_next):
    s, d = x.shape
    tm = _row_block(s)
    row = pl.BlockSpec((tm, d), lambda i: (i, 0))
    vec = pl.BlockSpec((1, d), lambda i: (0, 0))
    return pl.pallas_call(
        _resid_norm_kernel, grid=(s // tm,), in_specs=[row, row, vec, vec], out_specs=[row, row],
        out_shape=[jax.ShapeDtypeStruct((s, d), F32), jax.ShapeDtypeStruct((s, d), BF16)],
        compiler_params=_params("parallel"), name="resid_norm")(x, y, g_post.reshape(1, d), g_next.reshape(1, d))


def resid_only(x, y, g_post):
    s, d = x.shape
    tm = _row_block(s)
    row = pl.BlockSpec((tm, d), lambda i: (i, 0))
    vec = pl.BlockSpec((1, d), lambda i: (0, 0))
    return pl.pallas_call(
        _resid_kernel, grid=(s // tm,), in_specs=[row, row, vec], out_specs=row,
        out_shape=jax.ShapeDtypeStruct((s, d), F32), compiler_params=_params("parallel"),
        name="resid")(x, y, g_post.reshape(1, d))


def _mm_kernel(*refs, n_lhs, lhs_of_rhs, n_extra, epilogue):
    n_rhs = len(lhs_of_rhs)
    lhs_refs = refs[:n_lhs]
    rhs_refs = refs[n_lhs:n_lhs + n_rhs]
    extra_refs = refs[n_lhs + n_rhs:n_lhs + n_rhs + n_extra]
    out_refs = refs[n_lhs + n_rhs + n_extra:]
    lhs_vals = [r[...].astype(BF16) for r in lhs_refs]
    accs = [jnp.dot(lhs_vals[lhs_of_rhs[k]], rhs_refs[k][...].astype(BF16), preferred_element_type=F32)
            for k in range(n_rhs)]
    epilogue(accs, extra_refs, out_refs)


def fused_matmul(lhs, rhs, lhs_of_rhs, n_blocks, tm, tn, epilogue, outs, extras=(), name="mm"):
    m = lhs[0].shape[0]
    in_specs = [pl.BlockSpec((tm, a.shape[1]), lambda i, j: (i, 0)) for a in lhs]
    for arr, col in rhs:
        col_fn = col if callable(col) else functools.partial(lambda j, o: j + o, o=col)
        in_specs.append(pl.BlockSpec((arr.shape[0], tn), functools.partial(lambda i, j, f: (0, f(j)), f=col_fn)))
    for arr, bs, im in extras:
        in_specs.append(pl.BlockSpec(bs, im))
    out_specs = [pl.BlockSpec((tm, tn), im) for _, _, im in outs]
    out_shape = [jax.ShapeDtypeStruct((m, w), dt) for w, dt, _ in outs]
    body = functools.partial(_mm_kernel, n_lhs=len(lhs), lhs_of_rhs=tuple(lhs_of_rhs),
                             n_extra=len(extras), epilogue=epilogue)
    res = pl.pallas_call(
        body, grid=(m // tm, n_blocks), in_specs=in_specs, out_specs=out_specs, out_shape=out_shape,
        compiler_params=_params("parallel", "arbitrary"), name=name,
    )(*lhs, *[a for a, _ in rhs], *[a for a, _, _ in extras])
    return res


def _rope_tile(acc, cos, sin):
    parts = []
    for h in range(acc.shape[1] // HEAD_DIM):
        a = acc[:, h * HEAD_DIM:(h + 1) * HEAD_DIM]
        parts.append(a * cos + pltpu.roll(a, HEAD_DIM // 2, axis=1) * sin)
    return jnp.concatenate(parts, axis=1) if len(parts) > 1 else parts[0]


def _epi_plain(accs, extra, outs):
    outs[0][...] = accs[0].astype(outs[0].dtype)


def _epi_rope_where(rope_pred):
    def epi(accs, extra, outs):
        j = pl.program_id(1)
        cos_ref, sin_ref = extra

        @pl.when(rope_pred(j))
        def _():
            outs[0][...] = _rope_tile(accs[0], cos_ref[...], sin_ref[...]).astype(outs[0].dtype)

        @pl.when(jnp.logical_not(rope_pred(j)))
        def _():
            outs[0][...] = accs[0].astype(outs[0].dtype)
    return epi


def _epi_dual_rope(accs, extra, outs):
    cos_ref, sin_ref = extra
    outs[0][...] = accs[0].astype(outs[0].dtype)
    outs[1][...] = _rope_tile(accs[0], cos_ref[...], sin_ref[...]).astype(outs[1].dtype)


def _epi_merge(accs, extra, outs):
    ga, gb, ya, yb = accs
    outs[0][...] = (jax.nn.sigmoid(ga) * ya + jax.nn.sigmoid(gb) * yb).astype(outs[0].dtype)


def _epi_swiglu(accs, extra, outs):
    gt, up = accs
    outs[0][...] = (gt * jax.nn.sigmoid(gt) * up).astype(outs[0].dtype)


def _epi_ple(accs, extra, outs):
    pw, gl = accs
    outs[0][...] = (pw * jax.nn.sigmoid(gl)).astype(outs[0].dtype)


def _rope_tables(pos):
    half = HEAD_DIM // 2
    inv = ROPE_THETA ** (-jnp.arange(half, dtype=F32) / half)
    ang = pos.astype(F32)[:, None] * inv[None, :]
    cos, sin = jnp.cos(ang), jnp.sin(ang)
    return jnp.concatenate([cos, cos], axis=1), jnp.concatenate([-sin, sin], axis=1)


DIL_TQ = DIL_SPAN


def _dil_kernel(q_ref, kp_ref, kc_ref, vp_ref, vc_ref, o_ref, lse_ref, *, tiles_per_class):
    t = pl.program_id(0)
    prev_ok = lax.rem(t, tiles_per_class) > 0
    tq = DIL_TQ
    row = lax.broadcasted_iota(jnp.int32, (tq, 2 * tq), 0)
    col = lax.broadcasted_iota(jnp.int32, (tq, 2 * tq), 1)
    valid = (col >= row) & (col <= row + DIL_SPAN) & ((col >= tq) | prev_ok)
    for h in range(DIL_HEADS):
        sl = slice(h * HEAD_DIM, (h + 1) * HEAD_DIM)
        k = jnp.concatenate([kp_ref[:, sl], kc_ref[:, sl]], axis=0)
        v = jnp.concatenate([vp_ref[:, sl], vc_ref[:, sl]], axis=0)
        s = lax.dot_general(q_ref[:, sl], k, (((1,), (1,)), ((), ())), preferred_element_type=F32) * SCALE
        s = jnp.where(valid, s, NEG_INF)
        m = jnp.max(s, axis=-1, keepdims=True)
        e = jnp.exp(s - m)
        den = jnp.sum(e, axis=-1, keepdims=True)
        o_ref[:, sl] = jnp.dot((e / den).astype(BF16), v, preferred_element_type=F32)
        lse_ref[:, sl] = jnp.broadcast_to(m + jnp.log(den), (tq, HEAD_DIM))


def dilated_group_attention(qkv, rate):
    s = qkv.shape[0]
    tq = DIL_TQ
    tiles_per_class = s // rate // tq
    prev = lambda t: jnp.maximum(t - 1, 0)
    in_specs = [
        pl.BlockSpec((tq, DIL_W), lambda t: (t, 0)),
        pl.BlockSpec((tq, DIL_W), lambda t: (prev(t), 1)),
        pl.BlockSpec((tq, DIL_W), lambda t: (t, 1)),
        pl.BlockSpec((tq, DIL_W), lambda t: (prev(t), 2)),
        pl.BlockSpec((tq, DIL_W), lambda t: (t, 2)),
    ]
    out = pl.BlockSpec((tq, DIL_W), lambda t: (t, 0))
    return pl.pallas_call(
        functools.partial(_dil_kernel, tiles_per_class=tiles_per_class),
        grid=(s // tq,), in_specs=in_specs, out_specs=[out, out],
        out_shape=[jax.ShapeDtypeStruct((s, DIL_W), F32)] * 2,
        compiler_params=_params("parallel"), name=f"dilated_attn_r{rate}",
    )(qkv, qkv, qkv, qkv, qkv)


def _dil_combine_kernel(o0, o1, o2, l0, l1, l2, out_ref):
    la, lb, lc = l0[...], l1[...], l2[...]
    mx = jnp.maximum(jnp.maximum(la, lb), lc)
    wa, wb, wc = jnp.exp(la - mx), jnp.exp(lb - mx), jnp.exp(lc - mx)
    tot = wa + wb + wc
    out_ref[...] = ((wa / tot) * o0[...] + (wb / tot) * o1[...] + (wc / tot) * o2[...]).astype(out_ref.dtype)


def dilated_combine(os_, ls_):
    s = os_[0].shape[0]
    tm = min(512, s)
    blk = pl.BlockSpec((tm, DIL_W), lambda i: (i, 0))
    return pl.pallas_call(
        _dil_combine_kernel, grid=(s // tm,), in_specs=[blk] * 6, out_specs=blk,
        out_shape=jax.ShapeDtypeStruct((s, DIL_W), BF16), compiler_params=_params("parallel"),
        name="dilated_combine")(*os_, *ls_)


def _gelu_tanh(x):
    return 0.5 * x * (1.0 + jnp.tanh(0.7978845608028654 * (x + 0.044715 * (x * x * x))))


def _compress_kernel(x_ref, pe_ref, w1_ref, w2_ref, o_ref):
    x = x_ref[...]
    half = x.shape[1]
    lo = (x + pe_ref[0:1, :]).astype(BF16)
    hi = (x + pe_ref[1:2, :]).astype(BF16)
    h_lo = jnp.dot(lo, w1_ref[0:half, :].astype(BF16), preferred_element_type=F32)
    h_hi = jnp.dot(hi, w1_ref[half:2 * half, :].astype(BF16), preferred_element_type=F32)
    n = x.shape[0]
    hid = h_lo + pltpu.roll(h_hi, n - 1, axis=0)
    act = _gelu_tanh(hid).astype(BF16)
    o_ref[...] = jnp.dot(act, w2_ref[...].astype(BF16), preferred_element_type=F32).astype(o_ref.dtype)


def compress(kv, pe, w1, w2):
    s = kv.shape[0]
    n = s // CMP_STRIDE
    half = CMP_STRIDE * HEAD_DIM
    x = kv.reshape(n, CMP_STRIDE, NSA_KV, HEAD_DIM).transpose(2, 0, 1, 3).reshape(NSA_KV, n, half)
    pe2 = pe.reshape(2, half)
    return pl.pallas_call(
        _compress_kernel, grid=(NSA_KV,),
        in_specs=[pl.BlockSpec((None, n, half), lambda g: (g, 0, 0)),
                  pl.BlockSpec((2, half), lambda g: (0, 0)),
                  pl.BlockSpec(w1.shape, lambda g: (0, 0)),
                  pl.BlockSpec(w2.shape, lambda g: (0, 0))],
        out_specs=pl.BlockSpec((None, n, HEAD_DIM), lambda g: (g, 0, 0)),
        out_shape=jax.ShapeDtypeStruct((NSA_KV, n, HEAD_DIM), BF16),
        compiler_params=_params("parallel"), name="nsa_compress")(x, pe2, w1, w2)


NSA_TQ = 256
SEL_LANES = 128


def _nsa_select_kernel(q_ref, kc_ref, vc_ref, ov_ref, oc_ref, bias_ref, *, n_top):
    c = pl.program_id(1)
    tq = q_ref.shape[0]
    nck = kc_ref.shape[0]
    pos = c * tq + lax.broadcasted_iota(jnp.int32, (tq, 1), 0)
    n_idx = lax.broadcasted_iota(jnp.int32, (1, nck), 1)
    cvalid = (n_idx * CMP_STRIDE + (CMP_LEN - 1)) <= pos
    has_any = (pos >= CMP_LEN - 1).astype(F32)
    kc, vc, ov = kc_ref[...], vc_ref[...], ov_ref[...]
    imp = jnp.zeros((tq, SEL_LANES), F32)
    for j in range(NSA_QPG):
        sl = slice(j * HEAD_DIM, (j + 1) * HEAD_DIM)
        s = lax.dot_general(q_ref[:, sl], kc, (((1,), (1,)), ((), ())), preferred_element_type=F32) * SCALE
        s = jnp.where(cvalid, s, NEG_INF)
        m = jnp.max(s, axis=-1, keepdims=True)
        e = jnp.exp(s - m)
        p = ((e / jnp.sum(e, axis=-1, keepdims=True)) * has_any).astype(BF16)
        oc_ref[:, sl] = jnp.dot(p, vc, preferred_element_type=F32)
        imp = imp + jnp.dot(p, ov, preferred_element_type=F32)
    blk = lax.broadcasted_iota(jnp.int32, (1, SEL_LANES), 1)
    cur = lax.shift_right_logical(pos, SEL_LEN.bit_length() - 1)
    forced = ((blk == 0) | (blk == cur) | (blk == cur - 1)).astype(F32)
    svalid = blk * SEL_LEN <= pos
    work = jnp.where(svalid, imp + FORCE_BONUS * forced, NEG_INF)
    blk_f = jnp.broadcast_to(blk.astype(F32), (tq, SEL_LANES))
    chosen = jnp.zeros((tq, SEL_LANES), F32)
    for _ in range(n_top):
        mx = jnp.max(work, axis=-1, keepdims=True)
        first = jnp.min(jnp.where(work == mx, blk_f, float(SEL_LANES)), axis=-1, keepdims=True)
        hit = blk_f == first
        chosen = jnp.where(hit, 1.0, chosen)
        work = jnp.where(hit, -jnp.inf, work)
    keep = (chosen > 0.0) & svalid
    bias_ref[...] = jnp.where(keep, 0.0, NEG_INF / SCALE).astype(bias_ref.dtype)


def nsa_select(q_raw, k_c, v_c, ov, n_top):
    s = q_raw.shape[0]
    tq = min(NSA_TQ, s)
    nck = k_c.shape[1]
    return pl.pallas_call(
        functools.partial(_nsa_select_kernel, n_top=n_top), grid=(NSA_KV, s // tq),
        in_specs=[pl.BlockSpec((tq, NSA_QPG * HEAD_DIM), lambda g, c: (c, g)),
                  pl.BlockSpec((None, nck, HEAD_DIM), lambda g, c: (g, 0, 0)),
                  pl.BlockSpec((None, nck, HEAD_DIM), lambda g, c: (g, 0, 0)),
                  pl.BlockSpec((nck, SEL_LANES), lambda g, c: (0, 0))],
        out_specs=[pl.BlockSpec((tq, NSA_QPG * HEAD_DIM), lambda g, c: (c, g)),
                   pl.BlockSpec((None, tq, SEL_LANES), lambda g, c: (g, c, 0))],
        out_shape=[jax.ShapeDtypeStruct((s, B_W), F32),
                   jax.ShapeDtypeStruct((NSA_KV, s, SEL_LANES), BF16)],
        compiler_params=_params("parallel", "parallel"), name="nsa_select")(q_raw, k_c, v_c, ov)


def _nsa_attn_kernel(qr_ref, bias_ref, ks_ref, vs_ref, kw_ref, vw_ref, oc_ref, gate_ref, o_ref,
                     kaug_ref, qaug_ref, m_ref, l_ref, acc_ref):
    c = pl.program_id(1)
    tq = qr_ref.shape[0]
    s_len = ks_ref.shape[0]
    build_rows = min(512, s_len)

    @pl.when(c == 0)
    def _build_keys():
        def body(i, carry):
            r0 = pl.multiple_of(i * build_rows, build_rows)
            kaug_ref[pl.ds(r0, build_rows), 0:HEAD_DIM] = ks_ref[pl.ds(r0, build_rows), :]
            key = r0 + lax.broadcasted_iota(jnp.int32, (build_rows, SEL_LANES), 0)
            lane = lax.broadcasted_iota(jnp.int32, (build_rows, SEL_LANES), 1)
            onehot = lax.shift_right_logical(key, SEL_LEN.bit_length() - 1) == lane
            kaug_ref[pl.ds(r0, build_rows), HEAD_DIM:2 * HEAD_DIM] = jnp.where(onehot, 1.0, 0.0).astype(BF16)
            return carry
        lax.fori_loop(0, s_len // build_rows, body, 0)

    row = lax.broadcasted_iota(jnp.int32, (tq, tq), 0)
    col = lax.broadcasted_iota(jnp.int32, (tq, tq), 1)
    causal = col <= row
    nt = (((1,), (1,)), ((), ()))

    def reset():
        m_ref[...] = jnp.full(m_ref.shape, NEG_INF, F32)
        l_ref[...] = jnp.zeros(l_ref.shape, F32)
        acc_ref[...] = jnp.zeros(acc_ref.shape, F32)

    def online_step(j, s, v):
        m_prev = m_ref[j]
        m_new = jnp.maximum(m_prev, jnp.max(s, axis=-1, keepdims=True))
        alpha = jnp.exp(m_prev - m_new)
        p = jnp.exp(s - jnp.concatenate([m_new] * (tq // LANES), axis=1))
        l_ref[j] = alpha * l_ref[j] + jnp.sum(p, axis=-1, keepdims=True)
        acc_ref[j] = alpha * acc_ref[j] + jnp.dot(p.astype(BF16), v, preferred_element_type=F32)
        m_ref[j] = m_new

    def finish(j):
        return acc_ref[j] / l_ref[j]

    for j in range(NSA_QPG):
        qaug_ref[j, :, 0:HEAD_DIM] = qr_ref[:, j * HEAD_DIM:(j + 1) * HEAD_DIM]
        qaug_ref[j, :, HEAD_DIM:2 * HEAD_DIM] = bias_ref[...]
    reset()

    def sel_tile(kt, mask):
        k0 = pl.multiple_of(kt * tq, tq)
        kt_aug = kaug_ref[pl.ds(k0, tq), :]
        v = vs_ref[pl.ds(k0, tq), :]
        for j in range(NSA_QPG):
            s = lax.dot_general(qaug_ref[j], kt_aug, nt, preferred_element_type=F32) * SCALE
            if mask is not None:
                s = jnp.where(mask, s, NEG_INF)
            online_step(j, s, v)

    def sel_body(kt, carry):
        sel_tile(kt, None)
        return carry
    lax.fori_loop(0, c, sel_body, 0)
    sel_tile(c, causal)
    o_sel = [finish(j) for j in range(NSA_QPG)]

    reset()

    def win_tile(kt, mask):
        k0 = pl.multiple_of(kt * tq, tq)
        k = kw_ref[pl.ds(k0, tq), :]
        v = vw_ref[pl.ds(k0, tq), :]
        for j in range(NSA_QPG):
            q = qr_ref[:, j * HEAD_DIM:(j + 1) * HEAD_DIM]
            s = lax.dot_general(q, k, nt, preferred_element_type=F32) * SCALE
            if mask is not None:
                s = jnp.where(mask, s, NEG_INF)
            online_step(j, s, v)

    @pl.when(c >= 2)
    def _():
        win_tile(c - 2, col > row)

    @pl.when(c >= 1)
    def _():
        win_tile(c - 1, None)

    win_tile(c, causal)

    gates = jax.nn.sigmoid(gate_ref[...])
    for j in range(NSA_QPG):
        sl = slice(j * HEAD_DIM, (j + 1) * HEAD_DIM)
        g_c = gates[:, N_NSA_GATES * j + 0:N_NSA_GATES * j + 1]
        g_s = gates[:, N_NSA_GATES * j + 1:N_NSA_GATES * j + 2]
        g_w = gates[:, N_NSA_GATES * j + 2:N_NSA_GATES * j + 3]
        o_ref[:, sl] = (g_c * oc_ref[:, sl] + g_s * o_sel[j] + g_w * finish(j)).astype(o_ref.dtype)


def nsa_attention(q_rot, bias, kv4, o_c, gates):
    s = q_rot.shape[0]
    tq = min(NSA_TQ, s)
    assert WIN == 2 * tq
    qw = NSA_QPG * HEAD_DIM
    kvspec = lambda off: pl.BlockSpec((s, HEAD_DIM), functools.partial(lambda g, c, o: (0, o + g), o=off))
    ng = NSA_QPG * N_NSA_GATES
    return pl.pallas_call(
        _nsa_attn_kernel, grid=(NSA_KV, s // tq),
        in_specs=[pl.BlockSpec((tq, qw), lambda g, c: (c, g)),
                  pl.BlockSpec((None, tq, SEL_LANES), lambda g, c: (g, c, 0)),
                  kvspec(0), kvspec(NSA_KV), kvspec(2 * NSA_KV), kvspec(3 * NSA_KV),
                  pl.BlockSpec((tq, qw), lambda g, c: (c, g)),
                  pl.BlockSpec((None, tq, ng), lambda g, c: (g, c, 0))],
        out_specs=pl.BlockSpec((tq, qw), lambda g, c: (c, g)),
        out_shape=jax.ShapeDtypeStruct((s, B_W), BF16),
        scratch_shapes=[pltpu.VMEM((s, 2 * HEAD_DIM), BF16),
                        pltpu.VMEM((NSA_QPG, tq, 2 * HEAD_DIM), BF16),
                        pltpu.VMEM((NSA_QPG, tq, LANES), F32),
                        pltpu.VMEM((NSA_QPG, tq, LANES), F32),
                        pltpu.VMEM((NSA_QPG, tq, HEAD_DIM), F32)],
        compiler_params=_params("arbitrary", "arbitrary"), name="nsa_attention",
    )(q_rot, bias, kv4, kv4, kv4, kv4, o_c, gates)


def _class_major(a, rate):
    if rate == 1:
        return a
    s = a.shape[0]
    return a.reshape(s // rate, rate, -1).transpose(1, 0, 2).reshape(s, -1)


def _natural_order(a, rate):
    if rate == 1:
        return a
    s = a.shape[0]
    return a.reshape(rate, s // rate, -1).transpose(1, 0, 2).reshape(s, -1)


def _overlap_matrix(n_cmp_rows, n_sel):
    cs = jnp.arange(n_cmp_rows)[:, None] * CMP_STRIDE
    ss = jnp.arange(SEL_LANES)[None, :] * SEL_LEN
    ov = jnp.clip(jnp.minimum(cs + CMP_LEN, ss + SEL_LEN) - jnp.maximum(cs, ss), 0, None)
    ov = jnp.where(jnp.arange(SEL_LANES)[None, :] < n_sel, ov, 0)
    return (ov.astype(F32) / CMP_LEN).astype(BF16)


def _layer(x, p, g_mix_pre, w_in, pe_ck, w_ck1, w_ck2, pe_cv, w_cv1, w_cv2, w_a, w_b, w_out,
           g_mix_post, g_ffn_pre, w_gu, w_down, g_ffn_post, g_ple_pre, w_ple_gate, w_ple, g_ple_post):
    s, d = x.shape
    d_ff = w_down.shape[0]
    tm = min(1024, s)
    pos = jnp.arange(s)
    ij0 = lambda i, j: (i, 0)
    ijj = lambda i, j: (i, j)

    h = rmsnorm_bf16(x, g_mix_pre)

    tn = DIL_W
    o_groups, l_groups = [], []
    for gi, rate in enumerate(DIL_RATES):
        cos, sin = _rope_tables(_class_major(pos[:, None], rate)[:, 0])
        tab = [(cos, (tm, HEAD_DIM), ij0), (sin, (tm, HEAD_DIM), ij0)]
        qkv, = fused_matmul([_class_major(h, rate)], [(w_in, functools.partial(lambda j, g: g + N_DIL * j, g=gi))],
                            [0], 3, tm, tn, _epi_rope_where(lambda j: j < 2), [(3 * DIL_W, BF16, ijj)], tab,
                            name=f"proj_dil_r{rate}")
        o_g, l_g = dilated_group_attention(qkv, rate)
        o_groups.append(_natural_order(o_g, rate))
        l_groups.append(_natural_order(l_g, rate))
    attn_a = dilated_combine(o_groups, l_groups)

    cos, sin = _rope_tables(pos)
    tab = [(cos, (tm, HEAD_DIM), ij0), (sin, (tm, HEAD_DIM), ij0)]
    blk = lambda off: off // tn
    q_raw, q_rot = fused_matmul([h], [(w_in, blk(OFF_QB))], [0], B_W // tn, tm, tn, _epi_dual_rope,
                                [(B_W, BF16, ijj), (B_W, BF16, ijj)], tab, name="proj_nsa_q")
    kvc, = fused_matmul([h], [(w_in, blk(OFF_KC))], [0], 2 * KV_W // tn, tm, tn, _epi_plain,
                        [(2 * KV_W, F32, ijj)], name="proj_nsa_cmp")
    kv4, = fused_matmul([h], [(w_in, blk(OFF_KS))], [0], 4 * KV_W // tn, tm, tn,
                        _epi_rope_where(lambda j: lax.rem(j, 2) == 0), [(4 * KV_W, BF16, ijj)], tab,
                        name="proj_nsa_kv")
    gn, = fused_matmul([h], [(w_in, OFF_GN // LANES)], [0], 1, tm, LANES, _epi_plain,
                       [(LANES, F32, ijj)], name="proj_nsa_gates")
    gates = gn[:, :NSA_HEADS * N_NSA_GATES].reshape(s, NSA_KV, NSA_QPG * N_NSA_GATES).transpose(1, 0, 2)

    k_c = compress(kvc[:, :KV_W], pe_ck, w_ck1, w_ck2)
    v_c = compress(kvc[:, KV_W:], pe_cv, w_cv1, w_cv2)
    n_sel = s // SEL_LEN
    ov = _overlap_matrix(s // CMP_STRIDE, n_sel)
    o_c, bias = nsa_select(q_raw, k_c, v_c, ov, min(SEL_TOP, n_sel))
    attn_b = nsa_attention(q_rot, bias, kv4, o_c, gates)

    w_gm = w_in[:, OFF_GM:]
    tn_m = min(256, d)
    mixed, = fused_matmul([h, attn_a, attn_b], [(w_gm, 0), (w_gm, d // tn_m), (w_a, 0), (w_b, 0)], [0, 0, 1, 2],
                          d // tn_m, min(512, s), tn_m, _epi_merge, [(d, BF16, ijj)], name="merge")
    tn_o = min(512, d)
    mo, = fused_matmul([mixed], [(w_out, 0)], [0], d // tn_o, tm, tn_o, _epi_plain, [(d, F32, ijj)], name="out_proj")
    x1, h2 = resid_norm(x, mo, g_mix_post, g_ffn_pre)

    tn_f = 256
    act, = fused_matmul([h2], [(w_gu, 0), (w_gu, d_ff // tn_f)], [0, 0], d_ff // tn_f, tm, tn_f, _epi_swiglu,
                        [(d_ff, BF16, ijj)], name="ffn_up")
    tm_d = min(512, s)
    dn, = fused_matmul([act], [(w_down, 0)], [0], d // tn_f, tm_d, tn_f, _epi_plain, [(d, F32, ijj)], name="ffn_down")
    x2, h3 = resid_norm(x1, dn, g_ffn_post, g_ple_pre)

    ple, = fused_matmul([p, h3], [(w_ple, 0), (w_ple_gate, 0)], [0, 1], d // tn_o, tm, tn_o, _epi_ple,
                        [(d, F32, ijj)], name="ple")
    return resid_only(x2, ple, g_ple_post)


def kernel(x, p, g_mix_pre, w_in, pe_ck, w_ck1, w_ck2, pe_cv, w_cv1, w_cv2, w_a, w_b, w_out, g_mix_post,
           g_ffn_pre, w_gu, w_down, g_ffn_post, g_ple_pre, w_ple_gate, w_ple, g_ple_post):
    b, s, d = x.shape
    assert b == 1, "kernel supports batch 1"
    xs = x.reshape(s, d)
    for i in range(w_in.shape[0]):
        xs = _layer(xs, p[i].reshape(s, -1), g_mix_pre[i], w_in[i], pe_ck[i], w_ck1[i], w_ck2[i], pe_cv[i],
                    w_cv1[i], w_cv2[i], w_a[i], w_b[i], w_out[i], g_mix_post[i], g_ffn_pre[i], w_gu[i],
                    w_down[i], g_ffn_post[i], g_ple_pre[i], w_ple_gate[i], w_ple[i], g_ple_post[i])
    return xs.reshape(b, s, d)
```

```python
import functools

import jax
import jax.numpy as jnp
from jax import lax
from jax.experimental import pallas as pl
from jax.experimental.pallas import tpu as pltpu

F32 = jnp.float32
BF16 = jnp.bfloat16

HEAD_DIM = 128
LANES = 128
ROPE_THETA = 10000.0
NORM_EPS = 1e-6
NEG_INF = -1e30
SCALE = HEAD_DIM ** -0.5
LOG2E = 1.4426950408889634

DIL_RATES = (1, 4, 16)
DIL_HEADS = 4
DIL_SPAN = 128
DIL_W = DIL_HEADS * HEAD_DIM
N_DIL = len(DIL_RATES)
A_W = N_DIL * DIL_W

NSA_HEADS = 16
NSA_KV = 4
NSA_QPG = NSA_HEADS // NSA_KV
B_W = NSA_HEADS * HEAD_DIM
KV_W = NSA_KV * HEAD_DIM
CMP_LEN = 32
CMP_STRIDE = 16
SEL_LEN = 64
SEL_TOP = 16
WIN = 512
FORCE_BONUS = 1000.0
N_NSA_GATES = 3

OFF_QA, OFF_KA, OFF_VA = 0, A_W, 2 * A_W
OFF_QB = 3 * A_W
OFF_KC = OFF_QB + B_W
OFF_VC = OFF_KC + KV_W
OFF_KS = OFF_VC + KV_W
OFF_GN = OFF_KS + 4 * KV_W
OFF_GM = OFF_GN + NSA_HEADS * N_NSA_GATES

VMEM_LIMIT_BYTES = 56 * 1024 * 1024


def _params(*sem):
    return pltpu.CompilerParams(dimension_semantics=sem, vmem_limit_bytes=VMEM_LIMIT_BYTES)


def _rms(v, g):
    return v * lax.rsqrt(jnp.mean(v * v, axis=-1, keepdims=True) + NORM_EPS) * g


def _norm_kernel(x_ref, g_ref, h_ref):
    h_ref[...] = _rms(x_ref[...], g_ref[...]).astype(h_ref.dtype)


def _resid_norm_kernel(x_ref, y_ref, gp_ref, gn_ref, xo_ref, h_ref):
    xn = x_ref[...] + _rms(y_ref[...], gp_ref[...])
    xo_ref[...] = xn
    h_ref[...] = _rms(xn, gn_ref[...]).astype(h_ref.dtype)


def _resid_kernel(x_ref, y_ref, gp_ref, xo_ref):
    xo_ref[...] = x_ref[...] + _rms(y_ref[...], gp_ref[...])


def _row_block(s):
    return min(256, s)


def rmsnorm_bf16(x, g):
    s, d = x.shape
    tm = _row_block(s)
    row = pl.BlockSpec((tm, d), lambda i: (i, 0))
    vec = pl.BlockSpec((1, d), lambda i: (0, 0))
    return pl.pallas_call(
        _norm_kernel, grid=(s // tm,), in_specs=[row, vec], out_specs=row,
        out_shape=jax.ShapeDtypeStruct((s, d), BF16), compiler_params=_params("parallel"),
        name="rmsnorm")(x, g.reshape(1, d))


def resid_norm(x, y, g_post, g_next):
    s, d = x.shape
    tm = _row_block(s)
    row = pl.BlockSpec((tm, d), lambda i: (i, 0))
    vec = pl.BlockSpec((1, d), lambda i: (0, 0))
    return pl.pallas_call(
        _resid_norm_kernel, grid=(s // tm,), in_specs=[row, row, vec, vec], out_specs=[row, row],
        out_shape=[jax.ShapeDtypeStruct((s, d), F32), jax.ShapeDtypeStruct((s, d), BF16)],
        compiler_params=_params("parallel"), name="resid_norm")(x, y, g_post.reshape(1, d), g_next.reshape(1, d))


def resid_only(x, y, g_post):
    s, d = x.shape
    tm = _row_block(s)
    row = pl.BlockSpec((tm, d), lambda i: (i, 0))
    vec = pl.BlockSpec((1, d), lambda i: (0, 0))
    return pl.pallas_call(
        _resid_kernel, grid=(s // tm,), in_specs=[row, row, vec], out_specs=row,
        out_shape=jax.ShapeDtypeStruct((s, d), F32), compiler_params=_params("parallel"),
        name="resid")(x, y, g_post.reshape(1, d))


def _mm_kernel(*refs, n_lhs, lhs_of_rhs, n_extra, epilogue):
    n_rhs = len(lhs_of_rhs)
    lhs_refs = refs[:n_lhs]
    rhs_refs = refs[n_lhs:n_lhs + n_rhs]
    extra_refs = refs[n_lhs + n_rhs:n_lhs + n_rhs + n_extra]
    out_refs = refs[n_lhs + n_rhs + n_extra:]
    lhs_vals = [r[...].astype(BF16) for r in lhs_refs]
    accs = [jnp.dot(lhs_vals[lhs_of_rhs[k]], rhs_refs[k][...].astype(BF16), preferred_element_type=F32)
            for k in range(n_rhs)]
    epilogue(accs, extra_refs, out_refs)


def fused_matmul(lhs, rhs, lhs_of_rhs, n_blocks, tm, tn, epilogue, outs, extras=(), name="mm"):
    m = lhs[0].shape[0]
    in_specs = [pl.BlockSpec((tm, a.shape[1]), lambda i, j: (i, 0)) for a in lhs]
    for arr, col in rhs:
        col_fn = col if callable(col) else functools.partial(lambda j, o: j + o, o=col)
        in_specs.append(pl.BlockSpec((arr.shape[0], tn), functools.partial(lambda i, j, f: (0, f(j)), f=col_fn)))
    for arr, bs, im in extras:
        in_specs.append(pl.BlockSpec(bs, im))
    out_specs = [pl.BlockSpec((tm, tn), im) for _, _, im in outs]
    out_shape = [jax.ShapeDtypeStruct((m, w), dt) for w, dt, _ in outs]
    body = functools.partial(_mm_kernel, n_lhs=len(lhs), lhs_of_rhs=tuple(lhs_of_rhs),
                             n_extra=len(extras), epilogue=epilogue)
    res = pl.pallas_call(
        body, grid=(m // tm, n_blocks), in_specs=in_specs, out_specs=out_specs, out_shape=out_shape,
        compiler_params=_params("parallel", "arbitrary"), name=name,
    )(*lhs, *[a for a, _ in rhs], *[a for a, _, _ in extras])
    return res


def _rope_tile(acc, cos, sin):
    parts = []
    for h in range(acc.shape[1] // HEAD_DIM):
        a = acc[:, h * HEAD_DIM:(h + 1) * HEAD_DIM]
        parts.append(a * cos + pltpu.roll(a, HEAD_DIM // 2, axis=1) * sin)
    return jnp.concatenate(parts, axis=1) if len(parts) > 1 else parts[0]


def _epi_plain(accs, extra, outs):
    outs[0][...] = accs[0].astype(outs[0].dtype)


def _epi_rope_where(rope_pred):
    def epi(accs, extra, outs):
        j = pl.program_id(1)
        cos_ref, sin_ref = extra

        @pl.when(rope_pred(j))
        def _():
            outs[0][...] = _rope_tile(accs[0], cos_ref[...], sin_ref[...]).astype(outs[0].dtype)

        @pl.when(jnp.logical_not(rope_pred(j)))
        def _():
            outs[0][...] = accs[0].astype(outs[0].dtype)
    return epi


def _epi_dual_rope(accs, extra, outs):
    cos_ref, sin_ref = extra
    outs[0][...] = accs[0].astype(outs[0].dtype)
    outs[1][...] = (_rope_tile(accs[0], cos_ref[...], sin_ref[...]) * (SCALE * LOG2E)).astype(outs[1].dtype)


def _epi_merge(accs, extra, outs):
    ga, gb, ya, yb = accs
    outs[0][...] = (jax.nn.sigmoid(ga) * ya + jax.nn.sigmoid(gb) * yb).astype(outs[0].dtype)


def _epi_swiglu(accs, extra, outs):
    gt, up = accs
    outs[0][...] = (gt * jax.nn.sigmoid(gt) * up).astype(outs[0].dtype)


def _epi_ple(accs, extra, outs):
    pw, gl = accs
    outs[0][...] = (pw * jax.nn.sigmoid(gl)).astype(outs[0].dtype)


def _rope_tables(pos):
    half = HEAD_DIM // 2
    inv = ROPE_THETA ** (-jnp.arange(half, dtype=F32) / half)
    ang = pos.astype(F32)[:, None] * inv[None, :]
    cos, sin = jnp.cos(ang), jnp.sin(ang)
    return jnp.concatenate([cos, cos], axis=1), jnp.concatenate([-sin, sin], axis=1)


DIL_TQ = DIL_SPAN


def _dil_kernel(q_ref, kp_ref, kc_ref, vp_ref, vc_ref, o_ref, lse_ref, *, rate):
    prev_ok = pl.program_id(1) > 0
    tq = DIL_TQ
    row = lax.broadcasted_iota(jnp.int32, (tq, 2 * tq), 0)
    col = lax.broadcasted_iota(jnp.int32, (tq, 2 * tq), 1)
    valid = (col >= row) & (col <= row + DIL_SPAN) & ((col >= tq) | prev_ok)
    for a in range(rate):
        cls = pl.ds(a, tq, stride=rate) if rate > 1 else slice(None)
        q = q_ref[cls, :].astype(BF16)
        k = jnp.concatenate([kp_ref[cls, :], kc_ref[cls, :]], axis=0).astype(BF16)
        v = jnp.concatenate([vp_ref[cls, :], vc_ref[cls, :]], axis=0).astype(BF16)
        s = lax.dot_general(q, k, (((1,), (1,)), ((), ())), preferred_element_type=F32) * SCALE
        s = jnp.where(valid, s, NEG_INF)
        m = jnp.max(s, axis=-1, keepdims=True)
        e = jnp.exp(s - m)
        den = jnp.sum(e, axis=-1, keepdims=True)
        o_ref[cls, :] = jnp.dot((e / den).astype(BF16), v, preferred_element_type=F32)
        lse_ref[cls, :] = jnp.broadcast_to(m + jnp.log(den), (tq, HEAD_DIM))


def dilated_group_attention(qkv, group, rate):
    s = qkv.shape[0]
    rows = DIL_TQ * rate
    heads_total = A_W // HEAD_DIM
    col = lambda part: functools.partial(lambda h, t, p: p * heads_total + group * DIL_HEADS + h, p=part)
    cur = lambda part: pl.BlockSpec((rows, HEAD_DIM), functools.partial(lambda h, t, f: (t, f(h, t)), f=col(part)))
    prv = lambda part: pl.BlockSpec((rows, HEAD_DIM),
                                    functools.partial(lambda h, t, f: (jnp.maximum(t - 1, 0), f(h, t)), f=col(part)))
    out = pl.BlockSpec((rows, HEAD_DIM), lambda h, t: (t, h))
    return pl.pallas_call(
        functools.partial(_dil_kernel, rate=rate),
        grid=(DIL_HEADS, s // rows), in_specs=[cur(0), prv(1), cur(1), prv(2), cur(2)], out_specs=[out, out],
        out_shape=[jax.ShapeDtypeStruct((s, DIL_W), F32)] * 2,
        compiler_params=_params("parallel", "parallel"), name=f"dilated_attn_r{rate}",
    )(qkv, qkv, qkv, qkv, qkv)


def _dil_combine_kernel(o0, o1, o2, l0, l1, l2, out_ref):
    la, lb, lc = l0[...], l1[...], l2[...]
    mx = jnp.maximum(jnp.maximum(la, lb), lc)
    wa, wb, wc = jnp.exp(la - mx), jnp.exp(lb - mx), jnp.exp(lc - mx)
    tot = wa + wb + wc
    out_ref[...] = ((wa / tot) * o0[...] + (wb / tot) * o1[...] + (wc / tot) * o2[...]).astype(out_ref.dtype)


def dilated_combine(os_, ls_):
    s = os_[0].shape[0]
    tm = min(512, s)
    blk = pl.BlockSpec((tm, DIL_W), lambda i: (i, 0))
    return pl.pallas_call(
        _dil_combine_kernel, grid=(s // tm,), in_specs=[blk] * 6, out_specs=blk,
        out_shape=jax.ShapeDtypeStruct((s, DIL_W), BF16), compiler_params=_params("parallel"),
        name="dilated_combine")(*os_, *ls_)


def _gelu_tanh(x):
    return 0.5 * x * (1.0 + jnp.tanh(0.7978845608028654 * (x + 0.044715 * (x * x * x))))


def _compress_kernel(kv_ref, pe_ref, w1_ref, w2_ref, o_ref):
    n = kv_ref.shape[0] // CMP_STRIDE
    h_lo = jnp.zeros((n, w1_ref.shape[1]), F32)
    h_hi = jnp.zeros((n, w1_ref.shape[1]), F32)
    for l in range(CMP_STRIDE):
        x = kv_ref[pl.ds(l, n, stride=CMP_STRIDE), :]
        lo, hi = l, CMP_STRIDE + l
        h_lo = h_lo + jnp.dot((x + pe_ref[lo:lo + 1, :]).astype(BF16),
                              w1_ref[lo * HEAD_DIM:(lo + 1) * HEAD_DIM, :].astype(BF16), preferred_element_type=F32)
        h_hi = h_hi + jnp.dot((x + pe_ref[hi:hi + 1, :]).astype(BF16),
                              w1_ref[hi * HEAD_DIM:(hi + 1) * HEAD_DIM, :].astype(BF16), preferred_element_type=F32)
    hid = h_lo + pltpu.roll(h_hi, n - 1, axis=0)
    act = _gelu_tanh(hid).astype(BF16)
    o_ref[...] = jnp.dot(act, w2_ref[...].astype(BF16), preferred_element_type=F32).astype(o_ref.dtype)


def compress(kvc, first_col_block, pe, w1, w2):
    s = kvc.shape[0]
    n = s // CMP_STRIDE
    return pl.pallas_call(
        _compress_kernel, grid=(NSA_KV,),
        in_specs=[pl.BlockSpec((s, HEAD_DIM), lambda g: (0, first_col_block + g)),
                  pl.BlockSpec(pe.shape, lambda g: (0, 0)),
                  pl.BlockSpec(w1.shape, lambda g: (0, 0)),
                  pl.BlockSpec(w2.shape, lambda g: (0, 0))],
        out_specs=pl.BlockSpec((None, n, HEAD_DIM), lambda g: (g, 0, 0)),
        out_shape=jax.ShapeDtypeStruct((NSA_KV, n, HEAD_DIM), BF16),
        compiler_params=_params("parallel"), name="nsa_compress")(kvc, pe, w1, w2)


NSA_TQ = 512
NSA_CHAINS = 4
SEL_LANES = 128


def _nsa_select_kernel(q_ref, kc_ref, vc_ref, ov_ref, oc_ref, bias_ref, *, n_top):
    c = pl.program_id(1)
    tq = q_ref.shape[0]
    nck = kc_ref.shape[0]
    pos = c * tq + lax.broadcasted_iota(jnp.int32, (tq, 1), 0)
    n_idx = lax.broadcasted_iota(jnp.int32, (1, nck), 1)
    cvalid = (n_idx * CMP_STRIDE + (CMP_LEN - 1)) <= pos
    has_any = (pos >= CMP_LEN - 1).astype(F32)
    kc, vc, ov = kc_ref[...], vc_ref[...], ov_ref[...]
    imp = jnp.zeros((tq, SEL_LANES), F32)
    for j in range(NSA_QPG):
        sl = slice(j * HEAD_DIM, (j + 1) * HEAD_DIM)
        s = lax.dot_general(q_ref[:, sl], kc, (((1,), (1,)), ((), ())), preferred_element_type=F32) * SCALE
        s = jnp.where(cvalid, s, NEG_INF)
        m = jnp.max(s, axis=-1, keepdims=True)
        e = jnp.exp(s - m)
        p = ((e / jnp.sum(e, axis=-1, keepdims=True)) * has_any).astype(BF16)
        oc_ref[:, sl] = jnp.dot(p, vc, preferred_element_type=F32)
        imp = imp + jnp.dot(p, ov, preferred_element_type=F32)
    blk = lax.broadcasted_iota(jnp.int32, (1, SEL_LANES), 1)
    cur = lax.shift_right_logical(pos, SEL_LEN.bit_length() - 1)
    forced = ((blk == 0) | (blk == cur) | (blk == cur - 1)).astype(F32)
    svalid = blk * SEL_LEN <= pos
    work = jnp.where(svalid, imp + FORCE_BONUS * forced, NEG_INF)
    blk_f = jnp.broadcast_to(blk.astype(F32), (tq, SEL_LANES))
    chosen = jnp.zeros((tq, SEL_LANES), F32)
    for _ in range(n_top):
        mx = jnp.max(work, axis=-1, keepdims=True)
        first = jnp.min(jnp.where(work == mx, blk_f, float(SEL_LANES)), axis=-1, keepdims=True)
        hit = blk_f == first
        chosen = jnp.where(hit, 1.0, chosen)
        work = jnp.where(hit, -jnp.inf, work)
    keep = (chosen > 0.0) & svalid
    bias_ref[...] = jnp.where(keep, 0.0, NEG_INF / SCALE).astype(bias_ref.dtype)


def nsa_select(q_raw, k_c, v_c, ov, n_top):
    s = q_raw.shape[0]
    tq = min(NSA_TQ, s)
    nck = k_c.shape[1]
    return pl.pallas_call(
        functools.partial(_nsa_select_kernel, n_top=n_top), grid=(NSA_KV, s // tq),
        in_specs=[pl.BlockSpec((tq, NSA_QPG * HEAD_DIM), lambda g, c: (c, g)),
                  pl.BlockSpec((None, nck, HEAD_DIM), lambda g, c: (g, 0, 0)),
                  pl.BlockSpec((None, nck, HEAD_DIM), lambda g, c: (g, 0, 0)),
                  pl.BlockSpec((nck, SEL_LANES), lambda g, c: (0, 0))],
        out_specs=[pl.BlockSpec((tq, NSA_QPG * HEAD_DIM), lambda g, c: (c, g)),
                   pl.BlockSpec((None, tq, SEL_LANES), lambda g, c: (g, c, 0))],
        out_shape=[jax.ShapeDtypeStruct((s, B_W), F32),
                   jax.ShapeDtypeStruct((NSA_KV, s, SEL_LANES), BF16)],
        compiler_params=_params("parallel", "parallel"), name="nsa_select")(q_raw, k_c, v_c, ov)


def _nsa_attn_kernel(qr_ref, bias_ref, ks_ref, vs_ref, kw_ref, vw_ref, oc_ref, gate_ref, gate_t_ref, o_ref,
                     kaug_ref, vst_ref, vwt_ref, qaug_ref, sbuf_ref, m_ref, l_ref, acc_ref, m2_ref, l2_ref, acc2_ref):
    c = pl.program_id(1)
    tq = qr_ref.shape[0]
    nq = NSA_QPG * tq
    n_tiles = ks_ref.shape[0] // tq

    @pl.when(c == 0)
    def _build_keys():
        def body(i, carry):
            r0 = pl.multiple_of(i * tq, tq)
            kaug_ref[pl.ds(r0, tq), 0:HEAD_DIM] = ks_ref[pl.ds(r0, tq), :]
            key = r0 + lax.broadcasted_iota(jnp.int32, (tq, SEL_LANES), 0)
            lane = lax.broadcasted_iota(jnp.int32, (tq, SEL_LANES), 1)
            onehot = lax.shift_right_logical(key, SEL_LEN.bit_length() - 1) == lane
            kaug_ref[pl.ds(r0, tq), HEAD_DIM:2 * HEAD_DIM] = jnp.where(onehot, 1.0, 0.0).astype(BF16)
            vst_ref[i] = vs_ref[pl.ds(r0, tq), :].astype(F32).T.astype(BF16)
            vwt_ref[i] = vw_ref[pl.ds(r0, tq), :].astype(F32).T.astype(BF16)
            return carry
        lax.fori_loop(0, n_tiles, body, 0)

    for j in range(NSA_QPG):
        qaug_ref[j * tq:(j + 1) * tq, 0:HEAD_DIM] = qr_ref[:, j * HEAD_DIM:(j + 1) * HEAD_DIM]
        qaug_ref[j * tq:(j + 1) * tq, HEAD_DIM:2 * HEAD_DIM] = bias_ref[...]

    nl = nq // NSA_CHAINS
    lanes = [slice(h * nl, (h + 1) * nl) for h in range(NSA_CHAINS)]
    nt = (((1,), (1,)), ((), ()))
    for st in (m_ref, m2_ref):
        st[...] = jnp.full(st.shape, NEG_INF, F32)
    for st in (l_ref, l2_ref, acc_ref, acc2_ref):
        st[...] = jnp.zeros(st.shape, F32)

    def scores_into(slot, k_rows, q_cols, mask_fn):
        for ls in lanes:
            s_t = lax.dot_general(k_rows, qaug_ref[ls, q_cols], nt, preferred_element_type=F32)
            if mask_fn is not None:
                key_i = lax.broadcasted_iota(jnp.int32, (tq, nl), 0)
                qry_i = lax.broadcasted_iota(jnp.int32, (tq, nl), 1) & (tq - 1)
                s_t = jnp.where(mask_fn(key_i, qry_i), s_t, NEG_INF)
            sbuf_ref[slot, :, ls] = s_t

    def consume(slot, v_t, state):
        m_st, l_st, acc_st = state
        for ls in lanes:
            s_t = sbuf_ref[slot, :, ls]
            m_prev = m_st[:, ls]
            m_new = jnp.maximum(m_prev, jnp.max(s_t, axis=0, keepdims=True))
            alpha = jnp.exp2(m_prev - m_new)
            p_t = jnp.exp2(s_t - m_new)
            l_st[:, ls] = alpha * l_st[:, ls] + jnp.sum(p_t, axis=0, keepdims=True)
            acc_st[:, ls] = alpha * acc_st[:, ls] + jnp.dot(v_t, p_t.astype(BF16), preferred_element_type=F32)
            m_st[:, ls] = m_new

    sel_state = (m_ref, l_ref, acc_ref)
    win_state = (m2_ref, l2_ref, acc2_ref)
    aug = slice(0, 2 * HEAD_DIM)
    plain = slice(0, HEAD_DIM)
    causal = lambda k, q: k <= q
    row0 = pl.multiple_of(c * tq, tq)

    has_prev = c >= 1
    cp = jnp.maximum(c - 1, 0)
    rowp = pl.multiple_of(cp * tq, tq)
    scores_into(2, kw_ref[pl.ds(row0, tq), :], plain, causal)
    scores_into(3, kw_ref[pl.ds(rowp, tq), :], plain, lambda k, q: jnp.logical_and(k > q, has_prev))
    scores_into(0, kaug_ref[pl.ds(row0, tq), :], aug, causal)
    consume(2, vwt_ref[c], win_state)
    consume(3, vwt_ref[cp], win_state)

    def key_rows(k):
        return kaug_ref[pl.ds(pl.multiple_of((k - 1) * tq, tq), tq), :]

    def v_of(k):
        return vst_ref[jnp.where(k == 0, c, k - 1)]

    def sel_body(j, carry):
        k = 2 * j
        scores_into(1, key_rows(k + 1), aug, None)
        consume(0, v_of(k), sel_state)
        scores_into(0, key_rows(k + 2), aug, None)
        consume(1, v_of(k + 1), sel_state)
        return carry
    n_pairs = lax.shift_right_logical(c, 1)
    lax.fori_loop(0, n_pairs, sel_body, 0)
    odd = (c & 1) == 1

    @pl.when(odd)
    def _():
        scores_into(1, key_rows(c), aug, None)
        consume(0, v_of(c - 1), sel_state)
        consume(1, v_of(c), sel_state)

    @pl.when(jnp.logical_not(odd))
    def _():
        consume(0, v_of(c), sel_state)

    o_sel_t = acc_ref[...] / l_ref[...]
    o_win_t = acc2_ref[...] / l2_ref[...]

    gates = jax.nn.sigmoid(gate_ref[...])
    gates_t = jax.nn.sigmoid(gate_t_ref[...])
    for j in range(NSA_QPG):
        sl = slice(j * HEAD_DIM, (j + 1) * HEAD_DIM)
        ql = slice(j * tq, (j + 1) * tq)
        g_s = gates_t[N_NSA_GATES * j + 1:N_NSA_GATES * j + 2, :]
        g_w = gates_t[N_NSA_GATES * j + 2:N_NSA_GATES * j + 3, :]
        mixed_t = g_s * o_sel_t[:, ql] + g_w * o_win_t[:, ql]
        g_c = gates[:, N_NSA_GATES * j:N_NSA_GATES * j + 1]
        o_ref[:, sl] = (g_c * oc_ref[:, sl] + mixed_t.T).astype(o_ref.dtype)


def nsa_attention(q_rot, bias, kv4, o_c, gates):
    s = q_rot.shape[0]
    tq = min(NSA_TQ, s)
    assert WIN == tq
    qw = NSA_QPG * HEAD_DIM
    nq = NSA_QPG * tq
    kvspec = lambda off: pl.BlockSpec((s, HEAD_DIM), functools.partial(lambda g, c, o: (0, o + g), o=off))
    ng = NSA_QPG * N_NSA_GATES
    gates_t = gates.transpose(0, 2, 1)
    return pl.pallas_call(
        _nsa_attn_kernel, grid=(NSA_KV, s // tq),
        in_specs=[pl.BlockSpec((tq, qw), lambda g, c: (c, g)),
                  pl.BlockSpec((None, tq, SEL_LANES), lambda g, c: (g, c, 0)),
                  kvspec(0), kvspec(NSA_KV), kvspec(2 * NSA_KV), kvspec(3 * NSA_KV),
                  pl.BlockSpec((tq, qw), lambda g, c: (c, g)),
                  pl.BlockSpec((None, tq, ng), lambda g, c: (g, c, 0)),
                  pl.BlockSpec((None, ng, tq), lambda g, c: (g, 0, c))],
        out_specs=pl.BlockSpec((tq, qw), lambda g, c: (c, g)),
        out_shape=jax.ShapeDtypeStruct((s, B_W), BF16),
        scratch_shapes=[pltpu.VMEM((s, 2 * HEAD_DIM), BF16),
                        pltpu.VMEM((s // tq, HEAD_DIM, tq), BF16),
                        pltpu.VMEM((s // tq, HEAD_DIM, tq), BF16),
                        pltpu.VMEM((nq, 2 * HEAD_DIM), BF16),
                        pltpu.VMEM((4, tq, nq), F32),
                        pltpu.VMEM((1, nq), F32), pltpu.VMEM((1, nq), F32), pltpu.VMEM((HEAD_DIM, nq), F32),
                        pltpu.VMEM((1, nq), F32), pltpu.VMEM((1, nq), F32), pltpu.VMEM((HEAD_DIM, nq), F32)],
        compiler_params=_params("arbitrary", "arbitrary"), name="nsa_attention",
    )(q_rot, bias, kv4, kv4, kv4, kv4, o_c, gates, gates_t)


def _overlap_matrix(n_cmp_rows, n_sel):
    cs = jnp.arange(n_cmp_rows)[:, None] * CMP_STRIDE
    ss = jnp.arange(SEL_LANES)[None, :] * SEL_LEN
    ov = jnp.clip(jnp.minimum(cs + CMP_LEN, ss + SEL_LEN) - jnp.maximum(cs, ss), 0, None)
    ov = jnp.where(jnp.arange(SEL_LANES)[None, :] < n_sel, ov, 0)
    return (ov.astype(F32) / CMP_LEN).astype(BF16)


def _layer(x, p, g_mix_pre, w_in, pe_ck, w_ck1, w_ck2, pe_cv, w_cv1, w_cv2, w_a, w_b, w_out,
           g_mix_post, g_ffn_pre, w_gu, w_down, g_ffn_post, g_ple_pre, w_ple_gate, w_ple, g_ple_post):
    s, d = x.shape
    d_ff = w_down.shape[0]
    tm = min(1024, s)
    pos = jnp.arange(s)
    ij0 = lambda i, j: (i, 0)
    ijj = lambda i, j: (i, j)

    h = rmsnorm_bf16(x, g_mix_pre)

    cos, sin = _rope_tables(pos)
    tab = [(cos, (tm, HEAD_DIM), ij0), (sin, (tm, HEAD_DIM), ij0)]
    tn = DIL_W
    blk = lambda off: off // tn

    qkv_a, = fused_matmul([h], [(w_in, blk(OFF_QA))], [0], 3 * A_W // tn, tm, tn,
                          _epi_rope_where(lambda j: j < 2 * A_W // tn), [(3 * A_W, F32, ijj)], tab, name="proj_dil")
    groups = [dilated_group_attention(qkv_a, gi, rate) for gi, rate in enumerate(DIL_RATES)]
    attn_a = dilated_combine([o for o, _ in groups], [l for _, l in groups])

    q_raw, q_rot = fused_matmul([h], [(w_in, blk(OFF_QB))], [0], B_W // tn, tm, tn, _epi_dual_rope,
                                [(B_W, BF16, ijj), (B_W, BF16, ijj)], tab, name="proj_nsa_q")
    kvc, = fused_matmul([h], [(w_in, blk(OFF_KC))], [0], 2 * KV_W // tn, tm, tn, _epi_plain,
                        [(2 * KV_W, F32, ijj)], name="proj_nsa_cmp")
    kv4, = fused_matmul([h], [(w_in, blk(OFF_KS))], [0], 4 * KV_W // tn, tm, tn,
                        _epi_rope_where(lambda j: lax.rem(j, 2) == 0), [(4 * KV_W, BF16, ijj)], tab,
                        name="proj_nsa_kv")
    gn, = fused_matmul([h], [(w_in, OFF_GN // LANES)], [0], 1, tm, LANES, _epi_plain,
                       [(LANES, F32, ijj)], name="proj_nsa_gates")
    gates = gn[:, :NSA_HEADS * N_NSA_GATES].reshape(s, NSA_KV, NSA_QPG * N_NSA_GATES).transpose(1, 0, 2)

    k_c = compress(kvc, 0, pe_ck, w_ck1, w_ck2)
    v_c = compress(kvc, NSA_KV, pe_cv, w_cv1, w_cv2)
    n_sel = s // SEL_LEN
    ov = _overlap_matrix(s // CMP_STRIDE, n_sel)
    o_c, bias = nsa_select(q_raw, k_c, v_c, ov, min(SEL_TOP, n_sel))
    attn_b = nsa_attention(q_rot, bias, kv4, o_c, gates)

    w_gm = w_in[:, OFF_GM:]
    tn_m = min(256, d)
    mixed, = fused_matmul([h, attn_a, attn_b], [(w_gm, 0), (w_gm, d // tn_m), (w_a, 0), (w_b, 0)], [0, 0, 1, 2],
                          d // tn_m, min(512, s), tn_m, _epi_merge, [(d, BF16, ijj)], name="merge")
    tn_o = min(512, d)
    mo, = fused_matmul([mixed], [(w_out, 0)], [0], d // tn_o, tm, tn_o, _epi_plain, [(d, F32, ijj)], name="out_proj")
    x1, h2 = resid_norm(x, mo, g_mix_post, g_ffn_pre)

    tn_f = 256
    act, = fused_matmul([h2], [(w_gu, 0), (w_gu, d_ff // tn_f)], [0, 0], d_ff // tn_f, tm, tn_f, _epi_swiglu,
                        [(d_ff, BF16, ijj)], name="ffn_up")
    tm_d = min(512, s)
    dn, = fused_matmul([act], [(w_down, 0)], [0], d // tn_f, tm_d, tn_f, _epi_plain, [(d, F32, ijj)], name="ffn_down")
    x2, h3 = resid_norm(x1, dn, g_ffn_post, g_ple_pre)

    ple, = fused_matmul([p, h3], [(w_ple, 0), (w_ple_gate, 0)], [0, 1], d // tn_o, tm, tn_o, _epi_ple,
                        [(d, F32, ijj)], name="ple")
    return resid_only(x2, ple, g_ple_post)


def kernel(x, p, g_mix_pre, w_in, pe_ck, w_ck1, w_ck2, pe_cv, w_cv1, w_cv2, w_a, w_b, w_out, g_mix_post,
           g_ffn_pre, w_gu, w_down, g_ffn_post, g_ple_pre, w_ple_gate, w_ple, g_ple_post):
    b, s, d = x.shape
    assert b == 1, "kernel supports batch 1"
    xs = x.reshape(s, d)
    for i in range(w_in.shape[0]):
        xs = _layer(xs, p[i].reshape(s, -1), g_mix_pre[i], w_in[i], pe_ck[i], w_ck1[i], w_ck2[i], pe_cv[i],
                    w_cv1[i], w_cv2[i], w_a[i], w_b[i], w_out[i], g_mix_post[i], g_ffn_pre[i], w_gu[i],
                    w_down[i], g_ffn_post[i], g_ple_pre[i], w_ple_gate[i], w_ple[i], g_ple_post[i])
    return xs.reshape(b, s, d)
```

```python
import functools

import jax
import jax.numpy as jnp
from jax import lax
from jax.experimental import pallas as pl
from jax.experimental.pallas import tpu as pltpu

F32 = jnp.float32
BF16 = jnp.bfloat16

HEAD_DIM = 128
LANES = 128
ROPE_THETA = 10000.0
NORM_EPS = 1e-6
NEG_INF = -1e30
SCALE = HEAD_DIM ** -0.5
LOG2E = 1.4426950408889634

DIL_RATES = (1, 4, 16)
DIL_HEADS = 4
DIL_SPAN = 128
DIL_W = DIL_HEADS * HEAD_DIM
N_DIL = len(DIL_RATES)
A_W = N_DIL * DIL_W

NSA_HEADS = 16
NSA_KV = 4
NSA_QPG = NSA_HEADS // NSA_KV
B_W = NSA_HEADS * HEAD_DIM
KV_W = NSA_KV * HEAD_DIM
CMP_LEN = 32
CMP_STRIDE = 16
SEL_LEN = 64
SEL_TOP = 16
WIN = 512
FORCE_BONUS = 1000.0
N_NSA_GATES = 3

OFF_QA, OFF_KA, OFF_VA = 0, A_W, 2 * A_W
OFF_QB = 3 * A_W
OFF_KC = OFF_QB + B_W
OFF_VC = OFF_KC + KV_W
OFF_KS = OFF_VC + KV_W
OFF_GN = OFF_KS + 4 * KV_W
OFF_GM = OFF_GN + NSA_HEADS * N_NSA_GATES

VMEM_LIMIT_BYTES = 56 * 1024 * 1024


def _params(*sem):
    return pltpu.CompilerParams(dimension_semantics=sem, vmem_limit_bytes=VMEM_LIMIT_BYTES)


def _rms(v, g):
    return v * lax.rsqrt(jnp.mean(v * v, axis=-1, keepdims=True) + NORM_EPS) * g


def _norm_kernel(x_ref, g_ref, h_ref):
    h_ref[...] = _rms(x_ref[...], g_ref[...]).astype(h_ref.dtype)


def _resid_norm_kernel(x_ref, y_ref, gp_ref, gn_ref, xo_ref, h_ref):
    xn = x_ref[...] + _rms(y_ref[...], gp_ref[...])
    xo_ref[...] = xn
    h_ref[...] = _rms(xn, gn_ref[...]).astype(h_ref.dtype)


def _resid_kernel(x_ref, y_ref, gp_ref, xo_ref):
    xo_ref[...] = x_ref[...] + _rms(y_ref[...], gp_ref[...])


def _row_block(s):
    return min(256, s)


def rmsnorm_bf16(x, g):
    s, d = x.shape
    tm = _row_block(s)
    row = pl.BlockSpec((tm, d), lambda i: (i, 0))
    vec = pl.BlockSpec((1, d), lambda i: (0, 0))
    return pl.pallas_call(
        _norm_kernel, grid=(s // tm,), in_specs=[row, vec], out_specs=row,
        out_shape=jax.ShapeDtypeStruct((s, d), BF16), compiler_params=_params("parallel"),
        name="rmsnorm")(x, g.reshape(1, d))


def resid_norm(x, y, g_post, g_next):
    s, d = x.shape
    tm = _row_block(s)
    row = pl.BlockSpec((tm, d), lambda i: (i, 0))
    vec = pl.BlockSpec((1, d), lambda i: (0, 0))
    return pl.pallas_call(
        _resid_norm_kernel, grid=(s // tm,), in_specs=[row, row, vec, vec], out_specs=[row, row],
        out_shape=[jax.ShapeDtypeStruct((s, d), F32), jax.ShapeDtypeStruct((s, d), BF16)],
        compiler_params=_params("parallel"), name="resid_norm")(x, y, g_post.reshape(1, d), g_next.reshape(1, d))


def resid_only(x, y, g_post):
    s, d = x.shape
    tm = _row_block(s)
    row = pl.BlockSpec((tm, d), lambda i: (i, 0))
    vec = pl.BlockSpec((1, d), lambda i: (0, 0))
    return pl.pallas_call(
        _resid_kernel, grid=(s // tm,), in_specs=[row, row, vec], out_specs=row,
        out_shape=jax.ShapeDtypeStruct((s, d), F32), compiler_params=_params("parallel"),
        name="resid")(x, y, g_post.reshape(1, d))


def _mm_kernel(*refs, n_lhs, lhs_of_rhs, n_extra, epilogue):
    n_rhs = len(lhs_of_rhs)
    lhs_refs = refs[:n_lhs]
    rhs_refs = refs[n_lhs:n_lhs + n_rhs]
    extra_refs = refs[n_lhs + n_rhs:n_lhs + n_rhs + n_extra]
    out_refs = refs[n_lhs + n_rhs + n_extra:]
    lhs_vals = [r[...].astype(BF16) for r in lhs_refs]
    accs = [jnp.dot(lhs_vals[lhs_of_rhs[k]], rhs_refs[k][...].astype(BF16), preferred_element_type=F32)
            for k in range(n_rhs)]
    epilogue(accs, extra_refs, out_refs)


def fused_matmul(lhs, rhs, lhs_of_rhs, n_blocks, tm, tn, epilogue, outs, extras=(), name="mm", lhs_buffers=2):
    m = lhs[0].shape[0]
    in_specs = [pl.BlockSpec((tm, a.shape[1]), lambda i, j: (i, 0), pipeline_mode=pl.Buffered(lhs_buffers)) for a in lhs]
    for arr, col in rhs:
        col_fn = col if callable(col) else functools.partial(lambda j, o: j + o, o=col)
        in_specs.append(pl.BlockSpec((arr.shape[0], tn), functools.partial(lambda i, j, f: (0, f(j)), f=col_fn)))
    for arr, bs, im in extras:
        in_specs.append(pl.BlockSpec(bs, im))
    out_specs = [pl.BlockSpec((tm, tn), im) for _, _, im in outs]
    out_shape = [jax.ShapeDtypeStruct((m, w), dt) for w, dt, _ in outs]
    body = functools.partial(_mm_kernel, n_lhs=len(lhs), lhs_of_rhs=tuple(lhs_of_rhs),
                             n_extra=len(extras), epilogue=epilogue)
    res = pl.pallas_call(
        body, grid=(m // tm, n_blocks), in_specs=in_specs, out_specs=out_specs, out_shape=out_shape,
        compiler_params=_params("parallel", "arbitrary"), name=name,
    )(*lhs, *[a for a, _ in rhs], *[a for a, _, _ in extras])
    return res


def _rope_tile(acc, cos, sin):
    parts = []
    for h in range(acc.shape[1] // HEAD_DIM):
        a = acc[:, h * HEAD_DIM:(h + 1) * HEAD_DIM]
        parts.append(a * cos + pltpu.roll(a, HEAD_DIM // 2, axis=1) * sin)
    return jnp.concatenate(parts, axis=1) if len(parts) > 1 else parts[0]


def _epi_plain(accs, extra, outs):
    outs[0][...] = accs[0].astype(outs[0].dtype)


def _epi_rope_where(rope_pred):
    def epi(accs, extra, outs):
        j = pl.program_id(1)
        cos_ref, sin_ref = extra

        @pl.when(rope_pred(j))
        def _():
            outs[0][...] = _rope_tile(accs[0], cos_ref[...], sin_ref[...]).astype(outs[0].dtype)

        @pl.when(jnp.logical_not(rope_pred(j)))
        def _():
            outs[0][...] = accs[0].astype(outs[0].dtype)
    return epi


def _epi_dual_rope(accs, extra, outs):
    cos_ref, sin_ref = extra
    outs[0][...] = accs[0].astype(outs[0].dtype)
    outs[1][...] = (_rope_tile(accs[0], cos_ref[...], sin_ref[...]) * (SCALE * LOG2E)).astype(outs[1].dtype)


def _merge_kernel(h_ref, a_ref, b_ref, wga_ref, wgb_ref, wa_ref, wb_ref, o_ref, pa_ref, pb_ref, *, shift):
    j = pl.program_id(1)
    tn = o_ref.shape[1]
    h = h_ref[...]
    pa_ref[:, tn:] = jnp.dot(h, wga_ref[...].astype(BF16), preferred_element_type=F32)
    pb_ref[:, tn:] = jnp.dot(h, wgb_ref[...].astype(BF16), preferred_element_type=F32)

    @pl.when(j > 0)
    def _():
        ya = jnp.dot(a_ref[...], wa_ref[...].astype(BF16), preferred_element_type=F32)
        yb = jnp.dot(b_ref[...], wb_ref[...].astype(BF16), preferred_element_type=F32)
        gate_a = jax.nn.sigmoid(pa_ref[:, shift:shift + tn])
        gate_b = jax.nn.sigmoid(pb_ref[:, shift:shift + tn])
        o_ref[...] = (gate_a * ya + gate_b * yb).astype(o_ref.dtype)

    pa_ref[:, :tn] = pa_ref[:, tn:]
    pb_ref[:, :tn] = pb_ref[:, tn:]


def merge_branches(h, attn_a, attn_b, w_in, w_a, w_b, tm, tn):
    m, d = h.shape
    nb = d // tn
    base, shift = OFF_GM // tn, OFF_GM % tn
    assert shift > 0 and d % tn == 0
    prev = lambda j: jnp.maximum(j - 1, 0)
    resident = lambda a: pl.BlockSpec((tm, a.shape[1]), lambda i, j: (i, 0), pipeline_mode=pl.Buffered(1))
    return pl.pallas_call(
        functools.partial(_merge_kernel, shift=shift), grid=(m // tm, nb + 1),
        in_specs=[resident(h), resident(attn_a), resident(attn_b),
                  pl.BlockSpec((d, tn), lambda i, j: (0, base + j)),
                  pl.BlockSpec((d, tn), lambda i, j: (0, base + nb + j)),
                  pl.BlockSpec((w_a.shape[0], tn), lambda i, j: (0, prev(j))),
                  pl.BlockSpec((w_b.shape[0], tn), lambda i, j: (0, prev(j)))],
        out_specs=pl.BlockSpec((tm, tn), lambda i, j: (i, prev(j))),
        out_shape=jax.ShapeDtypeStruct((m, d), BF16),
        scratch_shapes=[pltpu.VMEM((tm, 2 * tn), F32), pltpu.VMEM((tm, 2 * tn), F32)],
        compiler_params=_params("parallel", "arbitrary"), name="merge",
    )(h, attn_a, attn_b, w_in, w_in, w_a, w_b)


def _epi_swiglu(accs, extra, outs):
    gt, up = accs
    outs[0][...] = (gt * jax.nn.sigmoid(gt) * up).astype(outs[0].dtype)


def _epi_ple(accs, extra, outs):
    pw, gl = accs
    outs[0][...] = (pw * jax.nn.sigmoid(gl)).astype(outs[0].dtype)


def _rope_tables(pos):
    half = HEAD_DIM // 2
    inv = ROPE_THETA ** (-jnp.arange(half, dtype=F32) / half)
    ang = pos.astype(F32)[:, None] * inv[None, :]
    cos, sin = jnp.cos(ang), jnp.sin(ang)
    return jnp.concatenate([cos, cos], axis=1), jnp.concatenate([-sin, sin], axis=1)


DIL_TQ = DIL_SPAN


def _dil_kernel(q_ref, kp_ref, kc_ref, vp_ref, vc_ref, o_ref, lse_ref, *, rate):
    prev_ok = pl.program_id(1) > 0
    tq = DIL_TQ
    n_sub = q_ref.shape[0] // (tq * rate)
    row = lax.broadcasted_iota(jnp.int32, (tq, 2 * tq), 0)
    col = lax.broadcasted_iota(jnp.int32, (tq, 2 * tq), 1)
    band = (col >= row) & (col <= row + DIL_SPAN)
    band_first = band & ((col >= tq) | prev_ok)

    def tile(a, i):
        start = a + rate * tq * i
        return pl.ds(start, tq, stride=rate) if rate > 1 else pl.ds(start, tq)

    for a in range(rate):
        for i in range(n_sub):
            cur = tile(a, i)
            if i == 0:
                kp, vp, valid = kp_ref[tile(a, n_sub - 1), :], vp_ref[tile(a, n_sub - 1), :], band_first
            else:
                kp, vp, valid = kc_ref[tile(a, i - 1), :], vc_ref[tile(a, i - 1), :], band
            q = q_ref[cur, :].astype(BF16)
            k = jnp.concatenate([kp, kc_ref[cur, :]], axis=0).astype(BF16)
            v = jnp.concatenate([vp, vc_ref[cur, :]], axis=0).astype(BF16)
            s = lax.dot_general(q, k, (((1,), (1,)), ((), ())), preferred_element_type=F32) * SCALE
            s = jnp.where(valid, s, NEG_INF)
            m = jnp.max(s, axis=-1, keepdims=True)
            e = jnp.exp(s - m)
            den = jnp.sum(e, axis=-1, keepdims=True)
            o_ref[cur, :] = jnp.dot((e / den).astype(BF16), v, preferred_element_type=F32)
            lse_ref[cur, :] = jnp.broadcast_to(m + jnp.log(den), (tq, HEAD_DIM))


def dilated_group_attention(qkv, group, rate):
    s = qkv.shape[0]
    rows = min(s, DIL_TQ * max(DIL_RATES))
    heads_total = A_W // HEAD_DIM
    col = lambda part: functools.partial(lambda h, t, p: p * heads_total + group * DIL_HEADS + h, p=part)
    cur = lambda part: pl.BlockSpec((rows, HEAD_DIM), functools.partial(lambda h, t, f: (t, f(h, t)), f=col(part)))
    prv = lambda part: pl.BlockSpec((rows, HEAD_DIM),
                                    functools.partial(lambda h, t, f: (jnp.maximum(t - 1, 0), f(h, t)), f=col(part)))
    out = pl.BlockSpec((rows, HEAD_DIM), lambda h, t: (t, h))
    return pl.pallas_call(
        functools.partial(_dil_kernel, rate=rate),
        grid=(DIL_HEADS, s // rows), in_specs=[cur(0), prv(1), cur(1), prv(2), cur(2)], out_specs=[out, out],
        out_shape=[jax.ShapeDtypeStruct((s, DIL_W), F32)] * 2,
        compiler_params=_params("parallel", "parallel"), name=f"dilated_attn_r{rate}",
    )(qkv, qkv, qkv, qkv, qkv)


def _dil_combine_kernel(o0, o1, o2, l0, l1, l2, out_ref):
    la, lb, lc = l0[...], l1[...], l2[...]
    mx = jnp.maximum(jnp.maximum(la, lb), lc)
    wa, wb, wc = jnp.exp(la - mx), jnp.exp(lb - mx), jnp.exp(lc - mx)
    tot = wa + wb + wc
    out_ref[...] = ((wa / tot) * o0[...] + (wb / tot) * o1[...] + (wc / tot) * o2[...]).astype(out_ref.dtype)


def dilated_combine(os_, ls_):
    s = os_[0].shape[0]
    tm = min(512, s)
    blk = pl.BlockSpec((tm, DIL_W), lambda i: (i, 0))
    return pl.pallas_call(
        _dil_combine_kernel, grid=(s // tm,), in_specs=[blk] * 6, out_specs=blk,
        out_shape=jax.ShapeDtypeStruct((s, DIL_W), BF16), compiler_params=_params("parallel"),
        name="dilated_combine")(*os_, *ls_)


def _gelu_tanh(x):
    return 0.5 * x * (1.0 + jnp.tanh(0.7978845608028654 * (x + 0.044715 * (x * x * x))))


def _compress_kernel(kv_ref, pe_ref, w1_ref, w2_ref, o_ref):
    n = kv_ref.shape[0] // CMP_STRIDE
    h_lo = jnp.zeros((n, w1_ref.shape[1]), F32)
    h_hi = jnp.zeros((n, w1_ref.shape[1]), F32)
    for l in range(CMP_STRIDE):
        x = kv_ref[pl.ds(l, n, stride=CMP_STRIDE), :]
        lo, hi = l, CMP_STRIDE + l
        h_lo = h_lo + jnp.dot((x + pe_ref[lo:lo + 1, :]).astype(BF16),
                              w1_ref[lo * HEAD_DIM:(lo + 1) * HEAD_DIM, :].astype(BF16), preferred_element_type=F32)
        h_hi = h_hi + jnp.dot((x + pe_ref[hi:hi + 1, :]).astype(BF16),
                              w1_ref[hi * HEAD_DIM:(hi + 1) * HEAD_DIM, :].astype(BF16), preferred_element_type=F32)
    hid = h_lo + pltpu.roll(h_hi, n - 1, axis=0)
    act = _gelu_tanh(hid).astype(BF16)
    o_ref[...] = jnp.dot(act, w2_ref[...].astype(BF16), preferred_element_type=F32).astype(o_ref.dtype)


def compress(kvc, first_col_block, pe, w1, w2):
    s = kvc.shape[0]
    n = s // CMP_STRIDE
    return pl.pallas_call(
        _compress_kernel, grid=(NSA_KV,),
        in_specs=[pl.BlockSpec((s, HEAD_DIM), lambda g: (0, first_col_block + g)),
                  pl.BlockSpec(pe.shape, lambda g: (0, 0)),
                  pl.BlockSpec(w1.shape, lambda g: (0, 0)),
                  pl.BlockSpec(w2.shape, lambda g: (0, 0))],
        out_specs=pl.BlockSpec((None, n, HEAD_DIM), lambda g: (g, 0, 0)),
        out_shape=jax.ShapeDtypeStruct((NSA_KV, n, HEAD_DIM), BF16),
        compiler_params=_params("parallel"), name="nsa_compress")(kvc, pe, w1, w2)


NSA_TQ = 512
NSA_CHAINS = 4
SEL_LANES = 128


def _nsa_select_kernel(q_ref, kc_ref, vc_ref, ov_ref, oc_ref, bias_ref, *, n_top):
    c = pl.program_id(1)
    tq = q_ref.shape[0]
    nck = kc_ref.shape[0]
    pos = c * tq + lax.broadcasted_iota(jnp.int32, (tq, 1), 0)
    n_idx = lax.broadcasted_iota(jnp.int32, (1, nck), 1)
    cvalid = (n_idx * CMP_STRIDE + (CMP_LEN - 1)) <= pos
    has_any = (pos >= CMP_LEN - 1).astype(F32)
    kc, vc, ov = kc_ref[...], vc_ref[...], ov_ref[...]
    imp = jnp.zeros((tq, SEL_LANES), F32)
    for j in range(NSA_QPG):
        sl = slice(j * HEAD_DIM, (j + 1) * HEAD_DIM)
        s = lax.dot_general(q_ref[:, sl], kc, (((1,), (1,)), ((), ())), preferred_element_type=F32) * SCALE
        s = jnp.where(cvalid, s, NEG_INF)
        m = jnp.max(s, axis=-1, keepdims=True)
        e = jnp.exp(s - m)
        p = ((e / jnp.sum(e, axis=-1, keepdims=True)) * has_any).astype(BF16)
        oc_ref[:, sl] = jnp.dot(p, vc, preferred_element_type=F32)
        imp = imp + jnp.dot(p, ov, preferred_element_type=F32)
    blk = lax.broadcasted_iota(jnp.int32, (1, SEL_LANES), 1)
    cur = lax.shift_right_logical(pos, SEL_LEN.bit_length() - 1)
    forced = ((blk == 0) | (blk == cur) | (blk == cur - 1)).astype(F32)
    svalid = blk * SEL_LEN <= pos
    work = jnp.where(svalid, imp + FORCE_BONUS * forced, NEG_INF)
    blk_f = jnp.broadcast_to(blk.astype(F32), (tq, SEL_LANES))
    chosen = jnp.zeros((tq, SEL_LANES), F32)
    for _ in range(n_top):
        mx = jnp.max(work, axis=-1, keepdims=True)
        first = jnp.min(jnp.where(work == mx, blk_f, float(SEL_LANES)), axis=-1, keepdims=True)
        hit = blk_f == first
        chosen = jnp.where(hit, 1.0, chosen)
        work = jnp.where(hit, -jnp.inf, work)
    keep = (chosen > 0.0) & svalid
    bias_ref[...] = jnp.where(keep, 0.0, NEG_INF / SCALE).astype(bias_ref.dtype)


def nsa_select(q_raw, k_c, v_c, ov, n_top):
    s = q_raw.shape[0]
    tq = min(NSA_TQ, s)
    nck = k_c.shape[1]
    return pl.pallas_call(
        functools.partial(_nsa_select_kernel, n_top=n_top), grid=(NSA_KV, s // tq),
        in_specs=[pl.BlockSpec((tq, NSA_QPG * HEAD_DIM), lambda g, c: (c, g)),
                  pl.BlockSpec((None, nck, HEAD_DIM), lambda g, c: (g, 0, 0)),
                  pl.BlockSpec((None, nck, HEAD_DIM), lambda g, c: (g, 0, 0)),
                  pl.BlockSpec((nck, SEL_LANES), lambda g, c: (0, 0))],
        out_specs=[pl.BlockSpec((tq, NSA_QPG * HEAD_DIM), lambda g, c: (c, g)),
                   pl.BlockSpec((None, tq, SEL_LANES), lambda g, c: (g, c, 0))],
        out_shape=[jax.ShapeDtypeStruct((s, B_W), F32),
                   jax.ShapeDtypeStruct((NSA_KV, s, SEL_LANES), BF16)],
        compiler_params=_params("parallel", "parallel"), name="nsa_select")(q_raw, k_c, v_c, ov)


def _nsa_attn_kernel(qr_ref, bias_ref, ks_ref, vs_ref, kw_ref, vw_ref, oc_ref, gate_ref, gate_t_ref, o_ref,
                     kaug_ref, vst_ref, vwt_ref, qaug_ref, sbuf_ref, m_ref, l_ref, acc_ref, m2_ref, l2_ref, acc2_ref):
    c = pl.program_id(1)
    tq = qr_ref.shape[0]
    nq = NSA_QPG * tq
    n_tiles = ks_ref.shape[0] // tq

    @pl.when(c == 0)
    def _build_keys():
        def body(i, carry):
            r0 = pl.multiple_of(i * tq, tq)
            kaug_ref[pl.ds(r0, tq), 0:HEAD_DIM] = ks_ref[pl.ds(r0, tq), :]
            key = r0 + lax.broadcasted_iota(jnp.int32, (tq, SEL_LANES), 0)
            lane = lax.broadcasted_iota(jnp.int32, (tq, SEL_LANES), 1)
            onehot = lax.shift_right_logical(key, SEL_LEN.bit_length() - 1) == lane
            kaug_ref[pl.ds(r0, tq), HEAD_DIM:2 * HEAD_DIM] = jnp.where(onehot, 1.0, 0.0).astype(BF16)
            vst_ref[i] = vs_ref[pl.ds(r0, tq), :].astype(F32).T.astype(BF16)
            vwt_ref[i] = vw_ref[pl.ds(r0, tq), :].astype(F32).T.astype(BF16)
            return carry
        lax.fori_loop(0, n_tiles, body, 0)

    for j in range(NSA_QPG):
        qaug_ref[j * tq:(j + 1) * tq, 0:HEAD_DIM] = qr_ref[:, j * HEAD_DIM:(j + 1) * HEAD_DIM]
        qaug_ref[j * tq:(j + 1) * tq, HEAD_DIM:2 * HEAD_DIM] = bias_ref[...]

    nl = nq // NSA_CHAINS
    lanes = [slice(h * nl, (h + 1) * nl) for h in range(NSA_CHAINS)]
    nt = (((1,), (1,)), ((), ()))
    for st in (m_ref, m2_ref):
        st[...] = jnp.full(st.shape, NEG_INF, F32)
    for st in (l_ref, l2_ref, acc_ref, acc2_ref):
        st[...] = jnp.zeros(st.shape, F32)

    def scores_into(slot, k_rows, q_cols, mask_fn):
        for ls in lanes:
            s_t = lax.dot_general(k_rows, qaug_ref[ls, q_cols], nt, preferred_element_type=F32)
            if mask_fn is not None:
                key_i = lax.broadcasted_iota(jnp.int32, (tq, nl), 0)
                qry_i = lax.broadcasted_iota(jnp.int32, (tq, nl), 1) & (tq - 1)
                s_t = jnp.where(mask_fn(key_i, qry_i), s_t, NEG_INF)
            sbuf_ref[slot, :, ls] = s_t

    def consume(slot, v_t, state):
        m_st, l_st, acc_st = state
        for ls in lanes:
            s_t = sbuf_ref[slot, :, ls]
            m_prev = m_st[:, ls]
            m_new = jnp.maximum(m_prev, jnp.max(s_t, axis=0, keepdims=True))
            alpha = jnp.exp2(m_prev - m_new)
            p_t = jnp.exp2(s_t - m_new)
            l_st[:, ls] = alpha * l_st[:, ls] + jnp.sum(p_t, axis=0, keepdims=True)
            acc_st[:, ls] = alpha * acc_st[:, ls] + jnp.dot(v_t, p_t.astype(BF16), preferred_element_type=F32)
            m_st[:, ls] = m_new

    sel_state = (m_ref, l_ref, acc_ref)
    win_state = (m2_ref, l2_ref, acc2_ref)
    aug = slice(0, 2 * HEAD_DIM)
    plain = slice(0, HEAD_DIM)
    causal = lambda k, q: k <= q
    row0 = pl.multiple_of(c * tq, tq)

    has_prev = c >= 1
    cp = jnp.maximum(c - 1, 0)
    rowp = pl.multiple_of(cp * tq, tq)
    scores_into(2, kw_ref[pl.ds(row0, tq), :], plain, causal)
    scores_into(3, kw_ref[pl.ds(rowp, tq), :], plain, lambda k, q: jnp.logical_and(k > q, has_prev))
    scores_into(0, kaug_ref[pl.ds(row0, tq), :], aug, causal)
    consume(2, vwt_ref[c], win_state)
    consume(3, vwt_ref[cp], win_state)

    def key_rows(k):
        return kaug_ref[pl.ds(pl.multiple_of((k - 1) * tq, tq), tq), :]

    def v_of(k):
        return vst_ref[jnp.where(k == 0, c, k - 1)]

    def sel_body(j, carry):
        k = 2 * j
        scores_into(1, key_rows(k + 1), aug, None)
        consume(0, v_of(k), sel_state)
        scores_into(0, key_rows(k + 2), aug, None)
        consume(1, v_of(k + 1), sel_state)
        return carry
    n_pairs = lax.shift_right_logical(c, 1)
    lax.fori_loop(0, n_pairs, sel_body, 0)
    odd = (c & 1) == 1

    @pl.when(odd)
    def _():
        scores_into(1, key_rows(c), aug, None)
        consume(0, v_of(c - 1), sel_state)
        consume(1, v_of(c), sel_state)

    @pl.when(jnp.logical_not(odd))
    def _():
        consume(0, v_of(c), sel_state)

    o_sel_t = acc_ref[...] / l_ref[...]
    o_win_t = acc2_ref[...] / l2_ref[...]

    gates = jax.nn.sigmoid(gate_ref[...])
    gates_t = jax.nn.sigmoid(gate_t_ref[...])
    for j in range(NSA_QPG):
        sl = slice(j * HEAD_DIM, (j + 1) * HEAD_DIM)
        ql = slice(j * tq, (j + 1) * tq)
        g_s = gates_t[N_NSA_GATES * j + 1:N_NSA_GATES * j + 2, :]
        g_w = gates_t[N_NSA_GATES * j + 2:N_NSA_GATES * j + 3, :]
        mixed_t = g_s * o_sel_t[:, ql] + g_w * o_win_t[:, ql]
        g_c = gates[:, N_NSA_GATES * j:N_NSA_GATES * j + 1]
        o_ref[:, sl] = (g_c * oc_ref[:, sl] + mixed_t.T).astype(o_ref.dtype)


def nsa_attention(q_rot, bias, kv4, o_c, gates):
    s = q_rot.shape[0]
    tq = min(NSA_TQ, s)
    assert WIN == tq
    qw = NSA_QPG * HEAD_DIM
    nq = NSA_QPG * tq
    kvspec = lambda off: pl.BlockSpec((s, HEAD_DIM), functools.partial(lambda g, c, o: (0, o + g), o=off))
    ng = NSA_QPG * N_NSA_GATES
    gates_t = gates.transpose(0, 2, 1)
    return pl.pallas_call(
        _nsa_attn_kernel, grid=(NSA_KV, s // tq),
        in_specs=[pl.BlockSpec((tq, qw), lambda g, c: (c, g)),
                  pl.BlockSpec((None, tq, SEL_LANES), lambda g, c: (g, c, 0)),
                  kvspec(0), kvspec(NSA_KV), kvspec(2 * NSA_KV), kvspec(3 * NSA_KV),
                  pl.BlockSpec((tq, qw), lambda g, c: (c, g)),
                  pl.BlockSpec((None, tq, ng), lambda g, c: (g, c, 0)),
                  pl.BlockSpec((None, ng, tq), lambda g, c: (g, 0, c))],
        out_specs=pl.BlockSpec((tq, qw), lambda g, c: (c, g)),
        out_shape=jax.ShapeDtypeStruct((s, B_W), BF16),
        scratch_shapes=[pltpu.VMEM((s, 2 * HEAD_DIM), BF16),
                        pltpu.VMEM((s // tq, HEAD_DIM, tq), BF16),
                        pltpu.VMEM((s // tq, HEAD_DIM, tq), BF16),
                        pltpu.VMEM((nq, 2 * HEAD_DIM), BF16),
                        pltpu.VMEM((4, tq, nq), F32),
                        pltpu.VMEM((1, nq), F32), pltpu.VMEM((1, nq), F32), pltpu.VMEM((HEAD_DIM, nq), F32),
                        pltpu.VMEM((1, nq), F32), pltpu.VMEM((1, nq), F32), pltpu.VMEM((HEAD_DIM, nq), F32)],
        compiler_params=_params("arbitrary", "arbitrary"), name="nsa_attention",
    )(q_rot, bias, kv4, kv4, kv4, kv4, o_c, gates, gates_t)


def _overlap_matrix(n_cmp_rows, n_sel):
    cs = jnp.arange(n_cmp_rows)[:, None] * CMP_STRIDE
    ss = jnp.arange(SEL_LANES)[None, :] * SEL_LEN
    ov = jnp.clip(jnp.minimum(cs + CMP_LEN, ss + SEL_LEN) - jnp.maximum(cs, ss), 0, None)
    ov = jnp.where(jnp.arange(SEL_LANES)[None, :] < n_sel, ov, 0)
    return (ov.astype(F32) / CMP_LEN).astype(BF16)


def _layer(x, p, g_mix_pre, w_in, pe_ck, w_ck1, w_ck2, pe_cv, w_cv1, w_cv2, w_a, w_b, w_out,
           g_mix_post, g_ffn_pre, w_gu, w_down, g_ffn_post, g_ple_pre, w_ple_gate, w_ple, g_ple_post):
    s, d = x.shape
    d_ff = w_down.shape[0]
    tm = min(1024, s)
    pos = jnp.arange(s)
    ij0 = lambda i, j: (i, 0)
    ijj = lambda i, j: (i, j)

    h = rmsnorm_bf16(x, g_mix_pre)

    cos, sin = _rope_tables(pos)
    tab = [(cos, (tm, HEAD_DIM), ij0), (sin, (tm, HEAD_DIM), ij0)]
    tn = DIL_W
    blk = lambda off: off // tn

    qkv_a, = fused_matmul([h], [(w_in, blk(OFF_QA))], [0], 3 * A_W // tn, tm, tn,
                          _epi_rope_where(lambda j: j < 2 * A_W // tn), [(3 * A_W, F32, ijj)], tab, name="proj_dil")
    groups = [dilated_group_attention(qkv_a, gi, rate) for gi, rate in enumerate(DIL_RATES)]
    attn_a = dilated_combine([o for o, _ in groups], [l for _, l in groups])

    q_raw, q_rot = fused_matmul([h], [(w_in, blk(OFF_QB))], [0], B_W // tn, tm, tn, _epi_dual_rope,
                                [(B_W, BF16, ijj), (B_W, BF16, ijj)], tab, name="proj_nsa_q")
    kvc, = fused_matmul([h], [(w_in, blk(OFF_KC))], [0], 2 * KV_W // tn, tm, tn, _epi_plain,
                        [(2 * KV_W, F32, ijj)], name="proj_nsa_cmp")
    kv4, = fused_matmul([h], [(w_in, blk(OFF_KS))], [0], 4 * KV_W // tn, tm, tn,
                        _epi_rope_where(lambda j: lax.rem(j, 2) == 0), [(4 * KV_W, BF16, ijj)], tab,
                        name="proj_nsa_kv")
    gn, = fused_matmul([h], [(w_in, OFF_GN // LANES)], [0], 1, tm, LANES, _epi_plain,
                       [(LANES, F32, ijj)], name="proj_nsa_gates")
    gates = gn[:, :NSA_HEADS * N_NSA_GATES].reshape(s, NSA_KV, NSA_QPG * N_NSA_GATES).transpose(1, 0, 2)

    k_c = compress(kvc, 0, pe_ck, w_ck1, w_ck2)
    v_c = compress(kvc, NSA_KV, pe_cv, w_cv1, w_cv2)
    n_sel = s // SEL_LEN
    ov = _overlap_matrix(s // CMP_STRIDE, n_sel)
    o_c, bias = nsa_select(q_raw, k_c, v_c, ov, min(SEL_TOP, n_sel))
    attn_b = nsa_attention(q_rot, bias, kv4, o_c, gates)

    mixed = merge_branches(h, attn_a, attn_b, w_in, w_a, w_b, tm, min(256, d))
    tn_o = min(512, d)
    mo, = fused_matmul([mixed], [(w_out, 0)], [0], d // tn_o, tm, tn_o, _epi_plain, [(d, F32, ijj)], name="out_proj")
    x1, h2 = resid_norm(x, mo, g_mix_post, g_ffn_pre)

    tn_f = 256
    act, = fused_matmul([h2], [(w_gu, 0), (w_gu, d_ff // tn_f)], [0, 0], d_ff // tn_f, tm, tn_f, _epi_swiglu,
                        [(d_ff, BF16, ijj)], name="ffn_up")
    dn, = fused_matmul([act], [(w_down, 0)], [0], d // tn_f, tm, tn_f, _epi_plain, [(d, F32, ijj)], name="ffn_down",
                       lhs_buffers=1)
    x2, h3 = resid_norm(x1, dn, g_ffn_post, g_ple_pre)

    ple, = fused_matmul([p, h3], [(w_ple, 0), (w_ple_gate, 0)], [0, 1], d // tn_o, tm, tn_o, _epi_ple,
                        [(d, F32, ijj)], name="ple")
    return resid_only(x2, ple, g_ple_post)


def kernel(x, p, g_mix_pre, w_in, pe_ck, w_ck1, w_ck2, pe_cv, w_cv1, w_cv2, w_a, w_b, w_out, g_mix_post,
           g_ffn_pre, w_gu, w_down, g_ffn_post, g_ple_pre, w_ple_gate, w_ple, g_ple_post):
    b, s, d = x.shape
    assert b == 1, "kernel supports batch 1"
    xs = x.reshape(s, d)
    for i in range(w_in.shape[0]):
        xs = _layer(xs, p[i].reshape(s, -1), g_mix_pre[i], w_in[i], pe_ck[i], w_ck1[i], w_ck2[i], pe_cv[i],
                    w_cv1[i], w_cv2[i], w_a[i], w_b[i], w_out[i], g_mix_post[i], g_ffn_pre[i], w_gu[i],
                    w_down[i], g_ffn_post[i], g_ple_pre[i], w_ple_gate[i], w_ple[i], g_ple_post[i])
    return xs.reshape(b, s, d)
```

```python
import functools

import jax
import jax.numpy as jnp
from jax import lax
from jax.experimental import pallas as pl
from jax.experimental.pallas import tpu as pltpu

F32 = jnp.float32
BF16 = jnp.bfloat16

HEAD_DIM = 128
LANES = 128
ROPE_THETA = 10000.0
NORM_EPS = 1e-6
NEG_INF = -1e30
SCALE = HEAD_DIM ** -0.5
LOG2E = 1.4426950408889634

DIL_RATES = (1, 4, 16)
DIL_HEADS = 4
DIL_SPAN = 128
DIL_W = DIL_HEADS * HEAD_DIM
N_DIL = len(DIL_RATES)
A_W = N_DIL * DIL_W

NSA_HEADS = 16
NSA_KV = 4
NSA_QPG = NSA_HEADS // NSA_KV
B_W = NSA_HEADS * HEAD_DIM
KV_W = NSA_KV * HEAD_DIM
CMP_LEN = 32
CMP_STRIDE = 16
SEL_LEN = 64
SEL_TOP = 16
WIN = 512
FORCE_BONUS = 1000.0
N_NSA_GATES = 3

OFF_QA, OFF_KA, OFF_VA = 0, A_W, 2 * A_W
OFF_QB = 3 * A_W
OFF_KC = OFF_QB + B_W
OFF_VC = OFF_KC + KV_W
OFF_KS = OFF_VC + KV_W
OFF_GN = OFF_KS + 4 * KV_W
OFF_GM = OFF_GN + NSA_HEADS * N_NSA_GATES

VMEM_LIMIT_BYTES = 56 * 1024 * 1024


def _params(*sem):
    return pltpu.CompilerParams(dimension_semantics=sem, vmem_limit_bytes=VMEM_LIMIT_BYTES)


def _rms(v, g):
    return v * lax.rsqrt(jnp.mean(v * v, axis=-1, keepdims=True) + NORM_EPS) * g


def _norm_kernel(x_ref, g_ref, h_ref):
    h_ref[...] = _rms(x_ref[...], g_ref[...]).astype(h_ref.dtype)


def _resid_norm_kernel(x_ref, y_ref, gp_ref, gn_ref, xo_ref, h_ref):
    xn = x_ref[...] + _rms(y_ref[...], gp_ref[...])
    xo_ref[...] = xn
    h_ref[...] = _rms(xn, gn_ref[...]).astype(h_ref.dtype)


def _resid_kernel(x_ref, y_ref, gp_ref, xo_ref):
    xo_ref[...] = x_ref[...] + _rms(y_ref[...], gp_ref[...])


def _row_block(s):
    return min(256, s)


def rmsnorm_bf16(x, g):
    s, d = x.shape
    tm = _row_block(s)
    row = pl.BlockSpec((tm, d), lambda i: (i, 0))
    vec = pl.BlockSpec((1, d), lambda i: (0, 0))
    return pl.pallas_call(
        _norm_kernel, grid=(s // tm,), in_specs=[row, vec], out_specs=row,
        out_shape=jax.ShapeDtypeStruct((s, d), BF16), compiler_params=_params("parallel"),
        name="rmsnorm")(x, g.reshape(1, d))


def resid_norm(x, y, g_post, g_next):
    s, d = x.shape
    tm = _row_block(s)
    row = pl.BlockSpec((tm, d), lambda i: (i, 0))
    vec = pl.BlockSpec((1, d), lambda i: (0, 0))
    return pl.pallas_call(
        _resid_norm_kernel, grid=(s // tm,), in_specs=[row, row, vec, vec], out_specs=[row, row],
        out_shape=[jax.ShapeDtypeStruct((s, d), F32), jax.ShapeDtypeStruct((s, d), BF16)],
        compiler_params=_params("parallel"), name="resid_norm")(x, y, g_post.reshape(1, d), g_next.reshape(1, d))


def resid_only(x, y, g_post):
    s, d = x.shape
    tm = _row_block(s)
    row = pl.BlockSpec((tm, d), lambda i: (i, 0))
    vec = pl.BlockSpec((1, d), lambda i: (0, 0))
    return pl.pallas_call(
        _resid_kernel, grid=(s // tm,), in_specs=[row, row, vec], out_specs=row,
        out_shape=jax.ShapeDtypeStruct((s, d), F32), compiler_params=_params("parallel"),
        name="resid")(x, y, g_post.reshape(1, d))


_NN = (((1,), (0,)), ((), ()))
_NT = (((1,), (1,)), ((), ()))


def _mm_kernel(*refs, n_lhs, lhs_of_rhs, n_extra, epilogue, rhs_t):
    n_rhs = len(lhs_of_rhs)
    lhs_refs = refs[:n_lhs]
    rhs_refs = refs[n_lhs:n_lhs + n_rhs]
    extra_refs = refs[n_lhs + n_rhs:n_lhs + n_rhs + n_extra]
    out_refs = refs[n_lhs + n_rhs + n_extra:]
    lhs_vals = [r[...].astype(BF16) for r in lhs_refs]
    accs = [lax.dot_general(lhs_vals[lhs_of_rhs[k]], rhs_refs[k][...].astype(BF16), _NT if rhs_t else _NN,
                            preferred_element_type=F32)
            for k in range(n_rhs)]
    epilogue(accs, extra_refs, out_refs)


def fused_matmul(lhs, rhs, lhs_of_rhs, n_blocks, tm, tn, epilogue, outs, extras=(), name="mm", lhs_buffers=2,
                 rhs_t=False):
    m = lhs[0].shape[0]
    in_specs = [pl.BlockSpec((tm, a.shape[1]), lambda i, j: (i, 0), pipeline_mode=pl.Buffered(lhs_buffers)) for a in lhs]
    for arr, col in rhs:
        col_fn = col if callable(col) else functools.partial(lambda j, o: j + o, o=col)
        if rhs_t:
            in_specs.append(pl.BlockSpec((tn, arr.shape[1]), functools.partial(lambda i, j, f: (f(j), 0), f=col_fn)))
        else:
            in_specs.append(pl.BlockSpec((arr.shape[0], tn), functools.partial(lambda i, j, f: (0, f(j)), f=col_fn)))
    for arr, bs, im in extras:
        in_specs.append(pl.BlockSpec(bs, im))
    out_specs = [pl.BlockSpec((tm, tn), im) for _, _, im in outs]
    out_shape = [jax.ShapeDtypeStruct((m, w), dt) for w, dt, _ in outs]
    body = functools.partial(_mm_kernel, n_lhs=len(lhs), lhs_of_rhs=tuple(lhs_of_rhs),
                             n_extra=len(extras), epilogue=epilogue, rhs_t=rhs_t)
    res = pl.pallas_call(
        body, grid=(m // tm, n_blocks), in_specs=in_specs, out_specs=out_specs, out_shape=out_shape,
        compiler_params=_params("parallel", "arbitrary"), name=name,
    )(*lhs, *[a for a, _ in rhs], *[a for a, _, _ in extras])
    return res


def _rope_tile(acc, cos, sin):
    parts = []
    for h in range(acc.shape[1] // HEAD_DIM):
        a = acc[:, h * HEAD_DIM:(h + 1) * HEAD_DIM]
        parts.append(a * cos + pltpu.roll(a, HEAD_DIM // 2, axis=1) * sin)
    return jnp.concatenate(parts, axis=1) if len(parts) > 1 else parts[0]


def _epi_plain(accs, extra, outs):
    outs[0][...] = accs[0].astype(outs[0].dtype)


def _epi_rope_where(rope_pred):
    def epi(accs, extra, outs):
        j = pl.program_id(1)
        cos_ref, sin_ref = extra

        @pl.when(rope_pred(j))
        def _():
            outs[0][...] = _rope_tile(accs[0], cos_ref[...], sin_ref[...]).astype(outs[0].dtype)

        @pl.when(jnp.logical_not(rope_pred(j)))
        def _():
            outs[0][...] = accs[0].astype(outs[0].dtype)
    return epi


def _epi_dual_rope(accs, extra, outs):
    cos_ref, sin_ref = extra
    outs[0][...] = accs[0].astype(outs[0].dtype)
    outs[1][...] = (_rope_tile(accs[0], cos_ref[...], sin_ref[...]) * (SCALE * LOG2E)).astype(outs[1].dtype)


def _merge_kernel(h_ref, a_ref, b_ref, wga_ref, wgb_ref, wa_ref, wb_ref, o_ref, pa_ref, pb_ref, *, shift):
    j = pl.program_id(1)
    tn = o_ref.shape[1]
    h = h_ref[...]
    pa_ref[:, tn:] = lax.dot_general(h, wga_ref[...].astype(BF16), _NT, preferred_element_type=F32)
    pb_ref[:, tn:] = lax.dot_general(h, wgb_ref[...].astype(BF16), _NT, preferred_element_type=F32)

    @pl.when(j > 0)
    def _():
        ya = jnp.dot(a_ref[...], wa_ref[...].astype(BF16), preferred_element_type=F32)
        yb = jnp.dot(b_ref[...], wb_ref[...].astype(BF16), preferred_element_type=F32)
        gate_a = jax.nn.sigmoid(pa_ref[:, shift:shift + tn])
        gate_b = jax.nn.sigmoid(pb_ref[:, shift:shift + tn])
        o_ref[...] = (gate_a * ya + gate_b * yb).astype(o_ref.dtype)

    pa_ref[:, :tn] = pa_ref[:, tn:]
    pb_ref[:, :tn] = pb_ref[:, tn:]


def merge_branches(h, attn_a, attn_b, w_in_t, w_a, w_b, tm, tn):
    m, d = h.shape
    nb = d // tn
    base, shift = OFF_GM // tn, OFF_GM % tn
    assert shift > 0 and d % tn == 0
    prev = lambda j: jnp.maximum(j - 1, 0)
    resident = lambda a: pl.BlockSpec((tm, a.shape[1]), lambda i, j: (i, 0), pipeline_mode=pl.Buffered(1))
    return pl.pallas_call(
        functools.partial(_merge_kernel, shift=shift), grid=(m // tm, nb + 1),
        in_specs=[resident(h), resident(attn_a), resident(attn_b),
                  pl.BlockSpec((tn, d), lambda i, j: (base + j, 0)),
                  pl.BlockSpec((tn, d), lambda i, j: (base + nb + j, 0)),
                  pl.BlockSpec((w_a.shape[0], tn), lambda i, j: (0, prev(j))),
                  pl.BlockSpec((w_b.shape[0], tn), lambda i, j: (0, prev(j)))],
        out_specs=pl.BlockSpec((tm, tn), lambda i, j: (i, prev(j))),
        out_shape=jax.ShapeDtypeStruct((m, d), BF16),
        scratch_shapes=[pltpu.VMEM((tm, 2 * tn), F32), pltpu.VMEM((tm, 2 * tn), F32)],
        compiler_params=_params("parallel", "arbitrary"), name="merge",
    )(h, attn_a, attn_b, w_in_t, w_in_t, w_a, w_b)


def _epi_swiglu(accs, extra, outs):
    gt, up = accs
    outs[0][...] = (gt * jax.nn.sigmoid(gt) * up).astype(outs[0].dtype)


def _epi_ple(accs, extra, outs):
    pw, gl = accs
    outs[0][...] = (pw * jax.nn.sigmoid(gl)).astype(outs[0].dtype)


def _rope_tables(pos):
    half = HEAD_DIM // 2
    inv = ROPE_THETA ** (-jnp.arange(half, dtype=F32) / half)
    ang = pos.astype(F32)[:, None] * inv[None, :]
    cos, sin = jnp.cos(ang), jnp.sin(ang)
    return jnp.concatenate([cos, cos], axis=1), jnp.concatenate([-sin, sin], axis=1)


DIL_TQ = DIL_SPAN


def _dil_kernel(q_ref, kp_ref, kc_ref, vp_ref, vc_ref, o_ref, lse_ref, *, rate):
    prev_ok = pl.program_id(1) > 0
    tq = DIL_TQ
    n_sub = q_ref.shape[0] // (tq * rate)
    row = lax.broadcasted_iota(jnp.int32, (tq, 2 * tq), 0)
    col = lax.broadcasted_iota(jnp.int32, (tq, 2 * tq), 1)
    band = (col >= row) & (col <= row + DIL_SPAN)
    band_first = band & ((col >= tq) | prev_ok)

    def tile(a, i):
        start = a + rate * tq * i
        return pl.ds(start, tq, stride=rate) if rate > 1 else pl.ds(start, tq)

    for a in range(rate):
        for i in range(n_sub):
            cur = tile(a, i)
            if i == 0:
                kp, vp, valid = kp_ref[tile(a, n_sub - 1), :], vp_ref[tile(a, n_sub - 1), :], band_first
            else:
                kp, vp, valid = kc_ref[tile(a, i - 1), :], vc_ref[tile(a, i - 1), :], band
            q = q_ref[cur, :].astype(BF16)
            k = jnp.concatenate([kp, kc_ref[cur, :]], axis=0).astype(BF16)
            v = jnp.concatenate([vp, vc_ref[cur, :]], axis=0).astype(BF16)
            s = lax.dot_general(q, k, (((1,), (1,)), ((), ())), preferred_element_type=F32) * SCALE
            s = jnp.where(valid, s, NEG_INF)
            m = jnp.max(s, axis=-1, keepdims=True)
            e = jnp.exp(s - m)
            den = jnp.sum(e, axis=-1, keepdims=True)
            o_ref[cur, :] = jnp.dot((e / den).astype(BF16), v, preferred_element_type=F32)
            lse_ref[cur, :] = jnp.broadcast_to(m + jnp.log(den), (tq, HEAD_DIM))


def dilated_group_attention(qkv, group, rate):
    s = qkv.shape[0]
    rows = min(s, DIL_TQ * max(DIL_RATES))
    heads_total = A_W // HEAD_DIM
    col = lambda part: functools.partial(lambda h, t, p: p * heads_total + group * DIL_HEADS + h, p=part)
    cur = lambda part: pl.BlockSpec((rows, HEAD_DIM), functools.partial(lambda h, t, f: (t, f(h, t)), f=col(part)))
    prv = lambda part: pl.BlockSpec((rows, HEAD_DIM),
                                    functools.partial(lambda h, t, f: (jnp.maximum(t - 1, 0), f(h, t)), f=col(part)))
    out = pl.BlockSpec((rows, HEAD_DIM), lambda h, t: (t, h))
    return pl.pallas_call(
        functools.partial(_dil_kernel, rate=rate),
        grid=(DIL_HEADS, s // rows), in_specs=[cur(0), prv(1), cur(1), prv(2), cur(2)], out_specs=[out, out],
        out_shape=[jax.ShapeDtypeStruct((s, DIL_W), F32)] * 2,
        compiler_params=_params("parallel", "parallel"), name=f"dilated_attn_r{rate}",
    )(qkv, qkv, qkv, qkv, qkv)


def _dil_combine_kernel(o0, o1, o2, l0, l1, l2, out_ref):
    la, lb, lc = l0[...], l1[...], l2[...]
    mx = jnp.maximum(jnp.maximum(la, lb), lc)
    wa, wb, wc = jnp.exp(la - mx), jnp.exp(lb - mx), jnp.exp(lc - mx)
    tot = wa + wb + wc
    out_ref[...] = ((wa / tot) * o0[...] + (wb / tot) * o1[...] + (wc / tot) * o2[...]).astype(out_ref.dtype)


def dilated_combine(os_, ls_):
    s = os_[0].shape[0]
    tm = min(512, s)
    blk = pl.BlockSpec((tm, DIL_W), lambda i: (i, 0))
    return pl.pallas_call(
        _dil_combine_kernel, grid=(s // tm,), in_specs=[blk] * 6, out_specs=blk,
        out_shape=jax.ShapeDtypeStruct((s, DIL_W), BF16), compiler_params=_params("parallel"),
        name="dilated_combine")(*os_, *ls_)


def _gelu_tanh(x):
    return 0.5 * x * (1.0 + jnp.tanh(0.7978845608028654 * (x + 0.044715 * (x * x * x))))


def _compress_kernel(kv_ref, pe_ref, w1_ref, w2_ref, o_ref):
    n = kv_ref.shape[0] // CMP_STRIDE
    h_lo = jnp.zeros((n, w1_ref.shape[1]), F32)
    h_hi = jnp.zeros((n, w1_ref.shape[1]), F32)
    for l in range(CMP_STRIDE):
        x = kv_ref[pl.ds(l, n, stride=CMP_STRIDE), :]
        lo, hi = l, CMP_STRIDE + l
        h_lo = h_lo + jnp.dot((x + pe_ref[lo:lo + 1, :]).astype(BF16),
                              w1_ref[lo * HEAD_DIM:(lo + 1) * HEAD_DIM, :].astype(BF16), preferred_element_type=F32)
        h_hi = h_hi + jnp.dot((x + pe_ref[hi:hi + 1, :]).astype(BF16),
                              w1_ref[hi * HEAD_DIM:(hi + 1) * HEAD_DIM, :].astype(BF16), preferred_element_type=F32)
    hid = h_lo + pltpu.roll(h_hi, n - 1, axis=0)
    act = _gelu_tanh(hid).astype(BF16)
    o_ref[...] = jnp.dot(act, w2_ref[...].astype(BF16), preferred_element_type=F32).astype(o_ref.dtype)


def compress(kvc, first_col_block, pe, w1, w2):
    s = kvc.shape[0]
    n = s // CMP_STRIDE
    return pl.pallas_call(
        _compress_kernel, grid=(NSA_KV,),
        in_specs=[pl.BlockSpec((s, HEAD_DIM), lambda g: (0, first_col_block + g)),
                  pl.BlockSpec(pe.shape, lambda g: (0, 0)),
                  pl.BlockSpec(w1.shape, lambda g: (0, 0)),
                  pl.BlockSpec(w2.shape, lambda g: (0, 0))],
        out_specs=pl.BlockSpec((None, n, HEAD_DIM), lambda g: (g, 0, 0)),
        out_shape=jax.ShapeDtypeStruct((NSA_KV, n, HEAD_DIM), BF16),
        compiler_params=_params("parallel"), name="nsa_compress")(kvc, pe, w1, w2)


NSA_TQ = 512
NSA_CHAINS = 4
SEL_LANES = 128


def _nsa_select_kernel(q_ref, kc_ref, vc_ref, ov_ref, oc_ref, bias_ref, *, n_top):
    c = pl.program_id(1)
    tq = q_ref.shape[0]
    nck = kc_ref.shape[0]
    pos = c * tq + lax.broadcasted_iota(jnp.int32, (tq, 1), 0)
    n_idx = lax.broadcasted_iota(jnp.int32, (1, nck), 1)
    cvalid = (n_idx * CMP_STRIDE + (CMP_LEN - 1)) <= pos
    has_any = (pos >= CMP_LEN - 1).astype(F32)
    kc, vc, ov = kc_ref[...], vc_ref[...], ov_ref[...]
    imp = jnp.zeros((tq, SEL_LANES), F32)
    for j in range(NSA_QPG):
        sl = slice(j * HEAD_DIM, (j + 1) * HEAD_DIM)
        s = lax.dot_general(q_ref[:, sl], kc, (((1,), (1,)), ((), ())), preferred_element_type=F32) * SCALE
        s = jnp.where(cvalid, s, NEG_INF)
        m = jnp.max(s, axis=-1, keepdims=True)
        e = jnp.exp(s - m)
        p = ((e / jnp.sum(e, axis=-1, keepdims=True)) * has_any).astype(BF16)
        oc_ref[:, sl] = jnp.dot(p, vc, preferred_element_type=F32)
        imp = imp + jnp.dot(p, ov, preferred_element_type=F32)
    blk = lax.broadcasted_iota(jnp.int32, (1, SEL_LANES), 1)
    cur = lax.shift_right_logical(pos, SEL_LEN.bit_length() - 1)
    forced = ((blk == 0) | (blk == cur) | (blk == cur - 1)).astype(F32)
    svalid = blk * SEL_LEN <= pos
    work = jnp.where(svalid, imp + FORCE_BONUS * forced, NEG_INF)
    blk_f = jnp.broadcast_to(blk.astype(F32), (tq, SEL_LANES))
    chosen = jnp.zeros((tq, SEL_LANES), F32)
    for _ in range(n_top):
        mx = jnp.max(work, axis=-1, keepdims=True)
        first = jnp.min(jnp.where(work == mx, blk_f, float(SEL_LANES)), axis=-1, keepdims=True)
        hit = blk_f == first
        chosen = jnp.where(hit, 1.0, chosen)
        work = jnp.where(hit, -jnp.inf, work)
    keep = (chosen > 0.0) & svalid
    bias_ref[...] = jnp.where(keep, 0.0, NEG_INF / SCALE).astype(bias_ref.dtype)


def nsa_select(q_raw, k_c, v_c, ov, n_top):
    s = q_raw.shape[0]
    tq = min(NSA_TQ, s)
    nck = k_c.shape[1]
    return pl.pallas_call(
        functools.partial(_nsa_select_kernel, n_top=n_top), grid=(NSA_KV, s // tq),
        in_specs=[pl.BlockSpec((tq, NSA_QPG * HEAD_DIM), lambda g, c: (c, g)),
                  pl.BlockSpec((None, nck, HEAD_DIM), lambda g, c: (g, 0, 0)),
                  pl.BlockSpec((None, nck, HEAD_DIM), lambda g, c: (g, 0, 0)),
                  pl.BlockSpec((nck, SEL_LANES), lambda g, c: (0, 0))],
        out_specs=[pl.BlockSpec((tq, NSA_QPG * HEAD_DIM), lambda g, c: (c, g)),
                   pl.BlockSpec((None, tq, SEL_LANES), lambda g, c: (g, c, 0))],
        out_shape=[jax.ShapeDtypeStruct((s, B_W), F32),
                   jax.ShapeDtypeStruct((NSA_KV, s, SEL_LANES), BF16)],
        compiler_params=_params("parallel", "parallel"), name="nsa_select")(q_raw, k_c, v_c, ov)


def _nsa_attn_kernel(qr_ref, bias_ref, ks_ref, vs_ref, kw_ref, vw_ref, oc_ref, gate_ref, gate_t_ref, o_ref,
                     kaug_ref, vst_ref, vwt_ref, qaug_ref, sbuf_ref, m_ref, l_ref, acc_ref, m2_ref, l2_ref, acc2_ref):
    c = pl.program_id(1)
    tq = qr_ref.shape[0]
    nq = NSA_QPG * tq
    n_tiles = ks_ref.shape[0] // tq

    @pl.when(c == 0)
    def _build_keys():
        def body(i, carry):
            r0 = pl.multiple_of(i * tq, tq)
            kaug_ref[pl.ds(r0, tq), 0:HEAD_DIM] = ks_ref[pl.ds(r0, tq), :]
            key = r0 + lax.broadcasted_iota(jnp.int32, (tq, SEL_LANES), 0)
            lane = lax.broadcasted_iota(jnp.int32, (tq, SEL_LANES), 1)
            onehot = lax.shift_right_logical(key, SEL_LEN.bit_length() - 1) == lane
            kaug_ref[pl.ds(r0, tq), HEAD_DIM:2 * HEAD_DIM] = jnp.where(onehot, 1.0, 0.0).astype(BF16)
            vst_ref[i] = vs_ref[pl.ds(r0, tq), :].astype(F32).T.astype(BF16)
            vwt_ref[i] = vw_ref[pl.ds(r0, tq), :].astype(F32).T.astype(BF16)
            return carry
        lax.fori_loop(0, n_tiles, body, 0)

    for j in range(NSA_QPG):
        qaug_ref[j * tq:(j + 1) * tq, 0:HEAD_DIM] = qr_ref[:, j * HEAD_DIM:(j + 1) * HEAD_DIM]
        qaug_ref[j * tq:(j + 1) * tq, HEAD_DIM:2 * HEAD_DIM] = bias_ref[...]

    nl = nq // NSA_CHAINS
    lanes = [slice(h * nl, (h + 1) * nl) for h in range(NSA_CHAINS)]
    nt = (((1,), (1,)), ((), ()))
    for st in (m_ref, m2_ref):
        st[...] = jnp.full(st.shape, NEG_INF, F32)
    for st in (l_ref, l2_ref, acc_ref, acc2_ref):
        st[...] = jnp.zeros(st.shape, F32)

    def scores_into(slot, k_rows, q_cols, mask_fn):
        for ls in lanes:
            s_t = lax.dot_general(k_rows, qaug_ref[ls, q_cols], nt, preferred_element_type=F32)
            if mask_fn is not None:
                key_i = lax.broadcasted_iota(jnp.int32, (tq, nl), 0)
                qry_i = lax.broadcasted_iota(jnp.int32, (tq, nl), 1) & (tq - 1)
                s_t = jnp.where(mask_fn(key_i, qry_i), s_t, NEG_INF)
            sbuf_ref[slot, :, ls] = s_t

    def consume(slot, v_t, state):
        m_st, l_st, acc_st = state
        for ls in lanes:
            s_t = sbuf_ref[slot, :, ls]
            m_prev = m_st[:, ls]
            m_new = jnp.maximum(m_prev, jnp.max(s_t, axis=0, keepdims=True))
            alpha = jnp.exp2(m_prev - m_new)
            p_t = jnp.exp2(s_t - m_new)
            l_st[:, ls] = alpha * l_st[:, ls] + jnp.sum(p_t, axis=0, keepdims=True)
            acc_st[:, ls] = alpha * acc_st[:, ls] + jnp.dot(v_t, p_t.astype(BF16), preferred_element_type=F32)
            m_st[:, ls] = m_new

    sel_state = (m_ref, l_ref, acc_ref)
    win_state = (m2_ref, l2_ref, acc2_ref)
    aug = slice(0, 2 * HEAD_DIM)
    plain = slice(0, HEAD_DIM)
    causal = lambda k, q: k <= q
    row0 = pl.multiple_of(c * tq, tq)

    has_prev = c >= 1
    cp = jnp.maximum(c - 1, 0)
    rowp = pl.multiple_of(cp * tq, tq)
    scores_into(2, kw_ref[pl.ds(row0, tq), :], plain, causal)
    scores_into(3, kw_ref[pl.ds(rowp, tq), :], plain, lambda k, q: jnp.logical_and(k > q, has_prev))
    scores_into(0, kaug_ref[pl.ds(row0, tq), :], aug, causal)
    consume(2, vwt_ref[c], win_state)
    consume(3, vwt_ref[cp], win_state)

    def key_rows(k):
        return kaug_ref[pl.ds(pl.multiple_of((k - 1) * tq, tq), tq), :]

    def v_of(k):
        return vst_ref[jnp.where(k == 0, c, k - 1)]

    def sel_body(j, carry):
        k = 2 * j
        scores_into(1, key_rows(k + 1), aug, None)
        consume(0, v_of(k), sel_state)
        scores_into(0, key_rows(k + 2), aug, None)
        consume(1, v_of(k + 1), sel_state)
        return carry
    n_pairs = lax.shift_right_logical(c, 1)
    lax.fori_loop(0, n_pairs, sel_body, 0)
    odd = (c & 1) == 1

    @pl.when(odd)
    def _():
        scores_into(1, key_rows(c), aug, None)
        consume(0, v_of(c - 1), sel_state)
        consume(1, v_of(c), sel_state)

    @pl.when(jnp.logical_not(odd))
    def _():
        consume(0, v_of(c), sel_state)

    o_sel_t = acc_ref[...] / l_ref[...]
    o_win_t = acc2_ref[...] / l2_ref[...]

    gates = jax.nn.sigmoid(gate_ref[...])
    gates_t = jax.nn.sigmoid(gate_t_ref[...])
    for j in range(NSA_QPG):
        sl = slice(j * HEAD_DIM, (j + 1) * HEAD_DIM)
        ql = slice(j * tq, (j + 1) * tq)
        g_s = gates_t[N_NSA_GATES * j + 1:N_NSA_GATES * j + 2, :]
        g_w = gates_t[N_NSA_GATES * j + 2:N_NSA_GATES * j + 3, :]
        mixed_t = g_s * o_sel_t[:, ql] + g_w * o_win_t[:, ql]
        g_c = gates[:, N_NSA_GATES * j:N_NSA_GATES * j + 1]
        o_ref[:, sl] = (g_c * oc_ref[:, sl] + mixed_t.T).astype(o_ref.dtype)


def nsa_attention(q_rot, bias, kv4, o_c, gates):
    s = q_rot.shape[0]
    tq = min(NSA_TQ, s)
    assert WIN == tq
    qw = NSA_QPG * HEAD_DIM
    nq = NSA_QPG * tq
    kvspec = lambda off: pl.BlockSpec((s, HEAD_DIM), functools.partial(lambda g, c, o: (0, o + g), o=off))
    ng = NSA_QPG * N_NSA_GATES
    gates_t = gates.transpose(0, 2, 1)
    return pl.pallas_call(
        _nsa_attn_kernel, grid=(NSA_KV, s // tq),
        in_specs=[pl.BlockSpec((tq, qw), lambda g, c: (c, g)),
                  pl.BlockSpec((None, tq, SEL_LANES), lambda g, c: (g, c, 0)),
                  kvspec(0), kvspec(NSA_KV), kvspec(2 * NSA_KV), kvspec(3 * NSA_KV),
                  pl.BlockSpec((tq, qw), lambda g, c: (c, g)),
                  pl.BlockSpec((None, tq, ng), lambda g, c: (g, c, 0)),
                  pl.BlockSpec((None, ng, tq), lambda g, c: (g, 0, c))],
        out_specs=pl.BlockSpec((tq, qw), lambda g, c: (c, g)),
        out_shape=jax.ShapeDtypeStruct((s, B_W), BF16),
        scratch_shapes=[pltpu.VMEM((s, 2 * HEAD_DIM), BF16),
                        pltpu.VMEM((s // tq, HEAD_DIM, tq), BF16),
                        pltpu.VMEM((s // tq, HEAD_DIM, tq), BF16),
                        pltpu.VMEM((nq, 2 * HEAD_DIM), BF16),
                        pltpu.VMEM((4, tq, nq), F32),
                        pltpu.VMEM((1, nq), F32), pltpu.VMEM((1, nq), F32), pltpu.VMEM((HEAD_DIM, nq), F32),
                        pltpu.VMEM((1, nq), F32), pltpu.VMEM((1, nq), F32), pltpu.VMEM((HEAD_DIM, nq), F32)],
        compiler_params=_params("arbitrary", "arbitrary"), name="nsa_attention",
    )(q_rot, bias, kv4, kv4, kv4, kv4, o_c, gates, gates_t)


def _overlap_matrix(n_cmp_rows, n_sel):
    cs = jnp.arange(n_cmp_rows)[:, None] * CMP_STRIDE
    ss = jnp.arange(SEL_LANES)[None, :] * SEL_LEN
    ov = jnp.clip(jnp.minimum(cs + CMP_LEN, ss + SEL_LEN) - jnp.maximum(cs, ss), 0, None)
    ov = jnp.where(jnp.arange(SEL_LANES)[None, :] < n_sel, ov, 0)
    return (ov.astype(F32) / CMP_LEN).astype(BF16)


def _layer(x, p, g_mix_pre, w_in, pe_ck, w_ck1, w_ck2, pe_cv, w_cv1, w_cv2, w_a, w_b, w_out,
           g_mix_post, g_ffn_pre, w_gu, w_down, g_ffn_post, g_ple_pre, w_ple_gate, w_ple, g_ple_post):
    s, d = x.shape
    d_ff = w_down.shape[0]
    tm = min(1024, s)
    pos = jnp.arange(s)
    ij0 = lambda i, j: (i, 0)
    ijj = lambda i, j: (i, j)

    h = rmsnorm_bf16(x, g_mix_pre)
    w_in_t = w_in.T

    cos, sin = _rope_tables(pos)
    tab = [(cos, (tm, HEAD_DIM), ij0), (sin, (tm, HEAD_DIM), ij0)]
    tn = DIL_W
    blk = lambda off: off // tn

    qkv_a, = fused_matmul([h], [(w_in_t,blk(OFF_QA))], [0], 3 * A_W // tn, tm, tn,
                          _epi_rope_where(lambda j: j < 2 * A_W // tn), [(3 * A_W, F32, ijj)], tab, name="proj_dil",
                          rhs_t=True)
    groups = [dilated_group_attention(qkv_a, gi, rate) for gi, rate in enumerate(DIL_RATES)]
    attn_a = dilated_combine([o for o, _ in groups], [l for _, l in groups])

    q_raw, q_rot = fused_matmul([h], [(w_in_t,blk(OFF_QB))], [0], B_W // tn, tm, tn, _epi_dual_rope,
                                [(B_W, BF16, ijj), (B_W, BF16, ijj)], tab, name="proj_nsa_q", rhs_t=True)
    kvc, = fused_matmul([h], [(w_in_t,blk(OFF_KC))], [0], 2 * KV_W // tn, tm, tn, _epi_plain,
                        [(2 * KV_W, F32, ijj)], name="proj_nsa_cmp", rhs_t=True)
    kv4, = fused_matmul([h], [(w_in_t,blk(OFF_KS))], [0], 4 * KV_W // tn, tm, tn,
                        _epi_rope_where(lambda j: lax.rem(j, 2) == 0), [(4 * KV_W, BF16, ijj)], tab,
                        name="proj_nsa_kv", rhs_t=True)
    gn, = fused_matmul([h], [(w_in_t,OFF_GN // LANES)], [0], 1, tm, LANES, _epi_plain,
                       [(LANES, F32, ijj)], name="proj_nsa_gates", rhs_t=True)
    gates = gn[:, :NSA_HEADS * N_NSA_GATES].reshape(s, NSA_KV, NSA_QPG * N_NSA_GATES).transpose(1, 0, 2)

    k_c = compress(kvc, 0, pe_ck, w_ck1, w_ck2)
    v_c = compress(kvc, NSA_KV, pe_cv, w_cv1, w_cv2)
    n_sel = s // SEL_LEN
    ov = _overlap_matrix(s // CMP_STRIDE, n_sel)
    o_c, bias = nsa_select(q_raw, k_c, v_c, ov, min(SEL_TOP, n_sel))
    attn_b = nsa_attention(q_rot, bias, kv4, o_c, gates)

    mixed = merge_branches(h, attn_a, attn_b, w_in_t, w_a, w_b, tm, min(256, d))
    tn_o = min(512, d)
    mo, = fused_matmul([mixed], [(w_out, 0)], [0], d // tn_o, tm, tn_o, _epi_plain, [(d, F32, ijj)], name="out_proj")
    x1, h2 = resid_norm(x, mo, g_mix_post, g_ffn_pre)

    tn_f = 256
    act, = fused_matmul([h2], [(w_gu, 0), (w_gu, d_ff // tn_f)], [0, 0], d_ff // tn_f, tm, tn_f, _epi_swiglu,
                        [(d_ff, BF16, ijj)], name="ffn_up")
    dn, = fused_matmul([act], [(w_down, 0)], [0], d // tn_f, tm, tn_f, _epi_plain, [(d, F32, ijj)], name="ffn_down",
                       lhs_buffers=1)
    x2, h3 = resid_norm(x1, dn, g_ffn_post, g_ple_pre)

    ple, = fused_matmul([p, h3], [(w_ple, 0), (w_ple_gate, 0)], [0, 1], d // tn_o, tm, tn_o, _epi_ple,
                        [(d, F32, ijj)], name="ple")
    return resid_only(x2, ple, g_ple_post)


def kernel(x, p, g_mix_pre, w_in, pe_ck, w_ck1, w_ck2, pe_cv, w_cv1, w_cv2, w_a, w_b, w_out, g_mix_post,
           g_ffn_pre, w_gu, w_down, g_ffn_post, g_ple_pre, w_ple_gate, w_ple, g_ple_post):
    b, s, d = x.shape
    assert b == 1, "kernel supports batch 1"
    xs = x.reshape(s, d)
    for i in range(w_in.shape[0]):
        xs = _layer(xs, p[i].reshape(s, -1), g_mix_pre[i], w_in[i], pe_ck[i], w_ck1[i], w_ck2[i], pe_cv[i],
                    w_cv1[i], w_cv2[i], w_a[i], w_b[i], w_out[i], g_mix_post[i], g_ffn_pre[i], w_gu[i],
                    w_down[i], g_ffn_post[i], g_ple_pre[i], w_ple_gate[i], w_ple[i], g_ple_post[i])
    return xs.reshape(b, s, d)
```

```python
import functools

import jax
import jax.numpy as jnp
from jax import lax
from jax.experimental import pallas as pl
from jax.experimental.pallas import tpu as pltpu

F32 = jnp.float32
BF16 = jnp.bfloat16

HEAD_DIM = 128
LANES = 128
ROPE_THETA = 10000.0
NORM_EPS = 1e-6
NEG_INF = -1e30
SCALE = HEAD_DIM ** -0.5
LOG2E = 1.4426950408889634

DIL_RATES = (1, 4, 16)
DIL_HEADS = 4
DIL_SPAN = 128
DIL_W = DIL_HEADS * HEAD_DIM
N_DIL = len(DIL_RATES)
A_W = N_DIL * DIL_W

NSA_HEADS = 16
NSA_KV = 4
NSA_QPG = NSA_HEADS // NSA_KV
B_W = NSA_HEADS * HEAD_DIM
KV_W = NSA_KV * HEAD_DIM
CMP_LEN = 32
CMP_STRIDE = 16
SEL_LEN = 64
SEL_TOP = 16
WIN = 512
FORCE_BONUS = 1000.0
N_NSA_GATES = 3

OFF_QA, OFF_KA, OFF_VA = 0, A_W, 2 * A_W
OFF_QB = 3 * A_W
OFF_KC = OFF_QB + B_W
OFF_VC = OFF_KC + KV_W
OFF_KS = OFF_VC + KV_W
OFF_GN = OFF_KS + 4 * KV_W
OFF_GM = OFF_GN + NSA_HEADS * N_NSA_GATES

VMEM_LIMIT_BYTES = 56 * 1024 * 1024


def _params(*sem):
    return pltpu.CompilerParams(dimension_semantics=sem, vmem_limit_bytes=VMEM_LIMIT_BYTES)


def _rms(v, g):
    return v * lax.rsqrt(jnp.mean(v * v, axis=-1, keepdims=True) + NORM_EPS) * g


def _norm_kernel(x_ref, g_ref, h_ref):
    h_ref[...] = _rms(x_ref[...], g_ref[...]).astype(h_ref.dtype)


def _resid_norm_kernel(x_ref, y_ref, gp_ref, gn_ref, xo_ref, h_ref):
    xn = x_ref[...] + _rms(y_ref[...], gp_ref[...])
    xo_ref[...] = xn
    h_ref[...] = _rms(xn, gn_ref[...]).astype(h_ref.dtype)


def _resid_kernel(x_ref, y_ref, gp_ref, xo_ref):
    xo_ref[...] = x_ref[...] + _rms(y_ref[...], gp_ref[...])


def _row_block(s):
    return min(256, s)


def rmsnorm_bf16(x, g):
    s, d = x.shape
    tm = _row_block(s)
    row = pl.BlockSpec((tm, d), lambda i: (i, 0))
    vec = pl.BlockSpec((1, d), lambda i: (0, 0))
    return pl.pallas_call(
        _norm_kernel, grid=(s // tm,), in_specs=[row, vec], out_specs=row,
        out_shape=jax.ShapeDtypeStruct((s, d), BF16), compiler_params=_params("parallel"),
        name="rmsnorm")(x, g.reshape(1, d))


def resid_norm(x, y, g_post, g_next):
    s, d = x.shape
    tm = _row_block(s)
    row = pl.BlockSpec((tm, d), lambda i: (i, 0))
    vec = pl.BlockSpec((1, d), lambda i: (0, 0))
    return pl.pallas_call(
        _resid_norm_kernel, grid=(s // tm,), in_specs=[row, row, vec, vec], out_specs=[row, row],
        out_shape=[jax.ShapeDtypeStruct((s, d), F32), jax.ShapeDtypeStruct((s, d), BF16)],
        compiler_params=_params("parallel"), name="resid_norm")(x, y, g_post.reshape(1, d), g_next.reshape(1, d))


def resid_only(x, y, g_post):
    s, d = x.shape
    tm = _row_block(s)
    row = pl.BlockSpec((tm, d), lambda i: (i, 0))
    vec = pl.BlockSpec((1, d), lambda i: (0, 0))
    return pl.pallas_call(
        _resid_kernel, grid=(s // tm,), in_specs=[row, row, vec], out_specs=row,
        out_shape=jax.ShapeDtypeStruct((s, d), F32), compiler_params=_params("parallel"),
        name="resid")(x, y, g_post.reshape(1, d))


_NN = (((1,), (0,)), ((), ()))
_NT = (((1,), (1,)), ((), ()))


def _mm_kernel(*refs, n_lhs, lhs_of_rhs, n_extra, epilogue, rhs_t):
    n_rhs = len(lhs_of_rhs)
    lhs_refs = refs[:n_lhs]
    rhs_refs = refs[n_lhs:n_lhs + n_rhs]
    extra_refs = refs[n_lhs + n_rhs:n_lhs + n_rhs + n_extra]
    out_refs = refs[n_lhs + n_rhs + n_extra:]
    lhs_vals = [r[...].astype(BF16) for r in lhs_refs]
    accs = [lax.dot_general(lhs_vals[lhs_of_rhs[k]], rhs_refs[k][...].astype(BF16), _NT if rhs_t else _NN,
                            preferred_element_type=F32)
            for k in range(n_rhs)]
    epilogue(accs, extra_refs, out_refs)


def fused_matmul(lhs, rhs, lhs_of_rhs, n_blocks, tm, tn, epilogue, outs, extras=(), name="mm", lhs_buffers=2,
                 rhs_t=False):
    m = lhs[0].shape[0]
    in_specs = [pl.BlockSpec((tm, a.shape[1]), lambda i, j: (i, 0), pipeline_mode=pl.Buffered(lhs_buffers)) for a in lhs]
    for arr, col in rhs:
        col_fn = col if callable(col) else functools.partial(lambda j, o: j + o, o=col)
        if rhs_t:
            in_specs.append(pl.BlockSpec((tn, arr.shape[1]), functools.partial(lambda i, j, f: (f(j), 0), f=col_fn)))
        else:
            in_specs.append(pl.BlockSpec((arr.shape[0], tn), functools.partial(lambda i, j, f: (0, f(j)), f=col_fn)))
    for arr, bs, im in extras:
        in_specs.append(pl.BlockSpec(bs, im))
    out_specs = [pl.BlockSpec((tm, tn), im) for _, _, im in outs]
    out_shape = [jax.ShapeDtypeStruct((m, w), dt) for w, dt, _ in outs]
    body = functools.partial(_mm_kernel, n_lhs=len(lhs), lhs_of_rhs=tuple(lhs_of_rhs),
                             n_extra=len(extras), epilogue=epilogue, rhs_t=rhs_t)
    res = pl.pallas_call(
        body, grid=(m // tm, n_blocks), in_specs=in_specs, out_specs=out_specs, out_shape=out_shape,
        compiler_params=_params("parallel", "arbitrary"), name=name,
    )(*lhs, *[a for a, _ in rhs], *[a for a, _, _ in extras])
    return res


def _rope_tile(acc, cos, sin):
    parts = []
    for h in range(acc.shape[1] // HEAD_DIM):
        a = acc[:, h * HEAD_DIM:(h + 1) * HEAD_DIM]
        parts.append(a * cos + pltpu.roll(a, HEAD_DIM // 2, axis=1) * sin)
    return jnp.concatenate(parts, axis=1) if len(parts) > 1 else parts[0]


def _epi_plain(accs, extra, outs):
    outs[0][...] = accs[0].astype(outs[0].dtype)


def _epi_rope_where(rope_pred):
    def epi(accs, extra, outs):
        j = pl.program_id(1)
        cos_ref, sin_ref = extra

        @pl.when(rope_pred(j))
        def _():
            outs[0][...] = _rope_tile(accs[0], cos_ref[...], sin_ref[...]).astype(outs[0].dtype)

        @pl.when(jnp.logical_not(rope_pred(j)))
        def _():
            outs[0][...] = accs[0].astype(outs[0].dtype)
    return epi


def _epi_dual_rope(accs, extra, outs):
    cos_ref, sin_ref = extra
    outs[0][...] = accs[0].astype(outs[0].dtype)
    outs[1][...] = (_rope_tile(accs[0], cos_ref[...], sin_ref[...]) * (SCALE * LOG2E)).astype(outs[1].dtype)


def _merge_kernel(h_ref, a_ref, b_ref, wga_ref, wgb_ref, wa_ref, wb_ref, o_ref, pa_ref, pb_ref, *, shift):
    j = pl.program_id(1)
    tn = o_ref.shape[1]
    h = h_ref[...]
    pa_ref[:, tn:] = lax.dot_general(h, wga_ref[...].astype(BF16), _NT, preferred_element_type=F32)
    pb_ref[:, tn:] = lax.dot_general(h, wgb_ref[...].astype(BF16), _NT, preferred_element_type=F32)

    @pl.when(j > 0)
    def _():
        ya = jnp.dot(a_ref[...], wa_ref[...].astype(BF16), preferred_element_type=F32)
        yb = jnp.dot(b_ref[...], wb_ref[...].astype(BF16), preferred_element_type=F32)
        gate_a = jax.nn.sigmoid(pa_ref[:, shift:shift + tn])
        gate_b = jax.nn.sigmoid(pb_ref[:, shift:shift + tn])
        o_ref[...] = (gate_a * ya + gate_b * yb).astype(o_ref.dtype)

    pa_ref[:, :tn] = pa_ref[:, tn:]
    pb_ref[:, :tn] = pb_ref[:, tn:]


def merge_branches(h, attn_a, attn_b, w_in_t, w_a, w_b, tm, tn):
    m, d = h.shape
    nb = d // tn
    base, shift = OFF_GM // tn, OFF_GM % tn
    assert shift > 0 and d % tn == 0
    prev = lambda j: jnp.maximum(j - 1, 0)
    resident = lambda a: pl.BlockSpec((tm, a.shape[1]), lambda i, j: (i, 0), pipeline_mode=pl.Buffered(1))
    return pl.pallas_call(
        functools.partial(_merge_kernel, shift=shift), grid=(m // tm, nb + 1),
        in_specs=[resident(h), resident(attn_a), resident(attn_b),
                  pl.BlockSpec((tn, d), lambda i, j: (base + j, 0)),
                  pl.BlockSpec((tn, d), lambda i, j: (base + nb + j, 0)),
                  pl.BlockSpec((w_a.shape[0], tn), lambda i, j: (0, prev(j))),
                  pl.BlockSpec((w_b.shape[0], tn), lambda i, j: (0, prev(j)))],
        out_specs=pl.BlockSpec((tm, tn), lambda i, j: (i, prev(j))),
        out_shape=jax.ShapeDtypeStruct((m, d), BF16),
        scratch_shapes=[pltpu.VMEM((tm, 2 * tn), F32), pltpu.VMEM((tm, 2 * tn), F32)],
        compiler_params=_params("parallel", "arbitrary"), name="merge",
    )(h, attn_a, attn_b, w_in_t, w_in_t, w_a, w_b)


def _epi_swiglu(accs, extra, outs):
    gt, up = accs
    outs[0][...] = (gt * jax.nn.sigmoid(gt) * up).astype(outs[0].dtype)


def _epi_ple(accs, extra, outs):
    pw, gl = accs
    outs[0][...] = (pw * jax.nn.sigmoid(gl)).astype(outs[0].dtype)


def _rope_tables(pos):
    half = HEAD_DIM // 2
    inv = ROPE_THETA ** (-jnp.arange(half, dtype=F32) / half)
    ang = pos.astype(F32)[:, None] * inv[None, :]
    cos, sin = jnp.cos(ang), jnp.sin(ang)
    return jnp.concatenate([cos, cos], axis=1), jnp.concatenate([-sin, sin], axis=1)


DIL_TQ = DIL_SPAN


def _dil_kernel(q_ref, kp_ref, kc_ref, vp_ref, vc_ref, o_ref, lse_ref, *, rate):
    prev_ok = pl.program_id(1) > 0
    tq = DIL_TQ
    n_sub = q_ref.shape[0] // (tq * rate)
    row = lax.broadcasted_iota(jnp.int32, (tq, 2 * tq), 0)
    col = lax.broadcasted_iota(jnp.int32, (tq, 2 * tq), 1)
    band = (col >= row) & (col <= row + DIL_SPAN)
    band_first = band & ((col >= tq) | prev_ok)

    def tile(a, i):
        start = a + rate * tq * i
        return pl.ds(start, tq, stride=rate) if rate > 1 else pl.ds(start, tq)

    for a in range(rate):
        for i in range(n_sub):
            cur = tile(a, i)
            if i == 0:
                kp, vp, valid = kp_ref[tile(a, n_sub - 1), :], vp_ref[tile(a, n_sub - 1), :], band_first
            else:
                kp, vp, valid = kc_ref[tile(a, i - 1), :], vc_ref[tile(a, i - 1), :], band
            q = q_ref[cur, :].astype(BF16)
            k = jnp.concatenate([kp, kc_ref[cur, :]], axis=0).astype(BF16)
            v = jnp.concatenate([vp, vc_ref[cur, :]], axis=0).astype(BF16)
            s = lax.dot_general(q, k, (((1,), (1,)), ((), ())), preferred_element_type=F32) * SCALE
            s = jnp.where(valid, s, NEG_INF)
            m = jnp.max(s, axis=-1, keepdims=True)
            e = jnp.exp(s - m)
            den = jnp.sum(e, axis=-1, keepdims=True)
            o_ref[cur, :] = jnp.dot((e / den).astype(BF16), v, preferred_element_type=F32)
            lse_ref[cur, :] = jnp.broadcast_to(m + jnp.log(den), (tq, HEAD_DIM))


def dilated_group_attention(qkv, group, rate):
    s = qkv.shape[0]
    rows = min(s, DIL_TQ * max(DIL_RATES))
    heads_total = A_W // HEAD_DIM
    col = lambda part: functools.partial(lambda h, t, p: p * heads_total + group * DIL_HEADS + h, p=part)
    cur = lambda part: pl.BlockSpec((rows, HEAD_DIM), functools.partial(lambda h, t, f: (t, f(h, t)), f=col(part)))
    prv = lambda part: pl.BlockSpec((rows, HEAD_DIM),
                                    functools.partial(lambda h, t, f: (jnp.maximum(t - 1, 0), f(h, t)), f=col(part)))
    out = pl.BlockSpec((rows, HEAD_DIM), lambda h, t: (t, h))
    return pl.pallas_call(
        functools.partial(_dil_kernel, rate=rate),
        grid=(DIL_HEADS, s // rows), in_specs=[cur(0), prv(1), cur(1), prv(2), cur(2)], out_specs=[out, out],
        out_shape=[jax.ShapeDtypeStruct((s, DIL_W), F32)] * 2,
        compiler_params=_params("parallel", "parallel"), name=f"dilated_attn_r{rate}",
    )(qkv, qkv, qkv, qkv, qkv)


def _dil_combine_kernel(o0, o1, o2, l0, l1, l2, out_ref):
    la, lb, lc = l0[...], l1[...], l2[...]
    mx = jnp.maximum(jnp.maximum(la, lb), lc)
    wa, wb, wc = jnp.exp(la - mx), jnp.exp(lb - mx), jnp.exp(lc - mx)
    tot = wa + wb + wc
    out_ref[...] = ((wa / tot) * o0[...] + (wb / tot) * o1[...] + (wc / tot) * o2[...]).astype(out_ref.dtype)


def dilated_combine(os_, ls_):
    s = os_[0].shape[0]
    tm = min(512, s)
    blk = pl.BlockSpec((tm, DIL_W), lambda i: (i, 0))
    return pl.pallas_call(
        _dil_combine_kernel, grid=(s // tm,), in_specs=[blk] * 6, out_specs=blk,
        out_shape=jax.ShapeDtypeStruct((s, DIL_W), BF16), compiler_params=_params("parallel"),
        name="dilated_combine")(*os_, *ls_)


def _gelu_tanh(x):
    return 0.5 * x * (1.0 + jnp.tanh(0.7978845608028654 * (x + 0.044715 * (x * x * x))))


def _compress_kernel(kv_ref, pe_ref, w1_ref, w2_ref, o_ref):
    n = kv_ref.shape[0] // CMP_STRIDE
    h_lo = jnp.zeros((n, w1_ref.shape[1]), F32)
    h_hi = jnp.zeros((n, w1_ref.shape[1]), F32)
    for l in range(CMP_STRIDE):
        x = kv_ref[pl.ds(l, n, stride=CMP_STRIDE), :]
        lo, hi = l, CMP_STRIDE + l
        h_lo = h_lo + jnp.dot((x + pe_ref[lo:lo + 1, :]).astype(BF16),
                              w1_ref[lo * HEAD_DIM:(lo + 1) * HEAD_DIM, :].astype(BF16), preferred_element_type=F32)
        h_hi = h_hi + jnp.dot((x + pe_ref[hi:hi + 1, :]).astype(BF16),
                              w1_ref[hi * HEAD_DIM:(hi + 1) * HEAD_DIM, :].astype(BF16), preferred_element_type=F32)
    hid = h_lo + pltpu.roll(h_hi, n - 1, axis=0)
    act = _gelu_tanh(hid).astype(BF16)
    o_ref[...] = jnp.dot(act, w2_ref[...].astype(BF16), preferred_element_type=F32).astype(o_ref.dtype)


def compress(kvc, first_col_block, pe, w1, w2):
    s = kvc.shape[0]
    n = s // CMP_STRIDE
    return pl.pallas_call(
        _compress_kernel, grid=(NSA_KV,),
        in_specs=[pl.BlockSpec((s, HEAD_DIM), lambda g: (0, first_col_block + g)),
                  pl.BlockSpec(pe.shape, lambda g: (0, 0)),
                  pl.BlockSpec(w1.shape, lambda g: (0, 0)),
                  pl.BlockSpec(w2.shape, lambda g: (0, 0))],
        out_specs=pl.BlockSpec((None, n, HEAD_DIM), lambda g: (g, 0, 0)),
        out_shape=jax.ShapeDtypeStruct((NSA_KV, n, HEAD_DIM), BF16),
        compiler_params=_params("parallel"), name="nsa_compress")(kvc, pe, w1, w2)


NSA_TQ = 512
NSA_CHAINS = 4
SEL_LANES = 128


def _nsa_select_kernel(q_ref, kc_ref, vc_ref, ov_ref, oc_ref, bias_ref, *, n_top):
    c = pl.program_id(1)
    tq = q_ref.shape[0]
    nck = kc_ref.shape[0]
    pos = c * tq + lax.broadcasted_iota(jnp.int32, (tq, 1), 0)
    n_idx = lax.broadcasted_iota(jnp.int32, (1, nck), 1)
    cvalid = (n_idx * CMP_STRIDE + (CMP_LEN - 1)) <= pos
    has_any = (pos >= CMP_LEN - 1).astype(F32)
    kc, vc, ov = kc_ref[...], vc_ref[...], ov_ref[...]
    imp = jnp.zeros((tq, SEL_LANES), F32)
    for j in range(NSA_QPG):
        sl = slice(j * HEAD_DIM, (j + 1) * HEAD_DIM)
        s = lax.dot_general(q_ref[:, sl], kc, (((1,), (1,)), ((), ())), preferred_element_type=F32) * SCALE
        s = jnp.where(cvalid, s, NEG_INF)
        m = jnp.max(s, axis=-1, keepdims=True)
        e = jnp.exp(s - m)
        p = ((e / jnp.sum(e, axis=-1, keepdims=True)) * has_any).astype(BF16)
        oc_ref[:, sl] = jnp.dot(p, vc, preferred_element_type=F32)
        imp = imp + jnp.dot(p, ov, preferred_element_type=F32)
    blk = lax.broadcasted_iota(jnp.int32, (1, SEL_LANES), 1)
    cur = lax.shift_right_logical(pos, SEL_LEN.bit_length() - 1)
    forced = ((blk == 0) | (blk == cur) | (blk == cur - 1)).astype(F32)
    svalid = blk * SEL_LEN <= pos
    work = jnp.where(svalid, imp + FORCE_BONUS * forced, NEG_INF)
    blk_f = jnp.broadcast_to(blk.astype(F32), (tq, SEL_LANES))
    chosen = jnp.zeros((tq, SEL_LANES), F32)
    for _ in range(n_top):
        mx = jnp.max(work, axis=-1, keepdims=True)
        first = jnp.min(jnp.where(work == mx, blk_f, float(SEL_LANES)), axis=-1, keepdims=True)
        hit = blk_f == first
        chosen = jnp.where(hit, 1.0, chosen)
        work = jnp.where(hit, -jnp.inf, work)
    keep = (chosen > 0.0) & svalid
    bias_ref[...] = jnp.where(keep, 0.0, NEG_INF / SCALE).astype(bias_ref.dtype)


def nsa_select(q_raw, k_c, v_c, ov, n_top):
    s = q_raw.shape[0]
    tq = min(NSA_TQ, s)
    nck = k_c.shape[1]
    return pl.pallas_call(
        functools.partial(_nsa_select_kernel, n_top=n_top), grid=(NSA_KV, s // tq),
        in_specs=[pl.BlockSpec((tq, NSA_QPG * HEAD_DIM), lambda g, c: (c, g)),
                  pl.BlockSpec((None, nck, HEAD_DIM), lambda g, c: (g, 0, 0)),
                  pl.BlockSpec((None, nck, HEAD_DIM), lambda g, c: (g, 0, 0)),
                  pl.BlockSpec((nck, SEL_LANES), lambda g, c: (0, 0))],
        out_specs=[pl.BlockSpec((tq, NSA_QPG * HEAD_DIM), lambda g, c: (c, g)),
                   pl.BlockSpec((None, tq, SEL_LANES), lambda g, c: (g, c, 0))],
        out_shape=[jax.ShapeDtypeStruct((s, B_W), F32),
                   jax.ShapeDtypeStruct((NSA_KV, s, SEL_LANES), BF16)],
        compiler_params=_params("parallel", "parallel"), name="nsa_select")(q_raw, k_c, v_c, ov)


def _nsa_attn_kernel(qr_ref, bias_ref, ks_ref, vs_ref, kw_ref, vw_ref, oc_ref, gate_ref, gate_t_ref, o_ref,
                     kaug_ref, vst_ref, vwt_ref, qaug_ref, sbuf_ref, m_ref, l_ref, acc_ref, m2_ref, l2_ref, acc2_ref):
    c = pl.program_id(1)
    tq = qr_ref.shape[0]
    nq = NSA_QPG * tq
    n_tiles = ks_ref.shape[0] // tq

    @pl.when(c == 0)
    def _build_keys():
        def body(i, carry):
            r0 = pl.multiple_of(i * tq, tq)
            kaug_ref[pl.ds(r0, tq), 0:HEAD_DIM] = ks_ref[pl.ds(r0, tq), :]
            key = r0 + lax.broadcasted_iota(jnp.int32, (tq, SEL_LANES), 0)
            lane = lax.broadcasted_iota(jnp.int32, (tq, SEL_LANES), 1)
            onehot = lax.shift_right_logical(key, SEL_LEN.bit_length() - 1) == lane
            kaug_ref[pl.ds(r0, tq), HEAD_DIM:2 * HEAD_DIM] = jnp.where(onehot, 1.0, 0.0).astype(BF16)
            vst_ref[i] = vs_ref[pl.ds(r0, tq), :].astype(F32).T.astype(BF16)
            vwt_ref[i] = vw_ref[pl.ds(r0, tq), :].astype(F32).T.astype(BF16)
            return carry
        lax.fori_loop(0, n_tiles, body, 0)

    for j in range(NSA_QPG):
        qaug_ref[j * tq:(j + 1) * tq, 0:HEAD_DIM] = qr_ref[:, j * HEAD_DIM:(j + 1) * HEAD_DIM]
        qaug_ref[j * tq:(j + 1) * tq, HEAD_DIM:2 * HEAD_DIM] = bias_ref[...]

    nl = nq // NSA_CHAINS
    lanes = [slice(h * nl, (h + 1) * nl) for h in range(NSA_CHAINS)]
    nt = (((1,), (1,)), ((), ()))
    for st in (m_ref, m2_ref):
        st[...] = jnp.full(st.shape, NEG_INF, F32)
    for st in (l_ref, l2_ref, acc_ref, acc2_ref):
        st[...] = jnp.zeros(st.shape, F32)

    def scores_into(slot, k_rows, q_cols, mask_fn):
        for ls in lanes:
            s_t = lax.dot_general(k_rows, qaug_ref[ls, q_cols], nt, preferred_element_type=F32)
            if mask_fn is not None:
                key_i = lax.broadcasted_iota(jnp.int32, (tq, nl), 0)
                qry_i = lax.broadcasted_iota(jnp.int32, (tq, nl), 1) & (tq - 1)
                s_t = jnp.where(mask_fn(key_i, qry_i), s_t, NEG_INF)
            sbuf_ref[slot, :, ls] = s_t

    def consume(slot, v_t, state):
        m_st, l_st, acc_st = state
        for ls in lanes:
            s_t = sbuf_ref[slot, :, ls]
            m_prev = m_st[:, ls]
            m_new = jnp.maximum(m_prev, jnp.max(s_t, axis=0, keepdims=True))
            alpha = jnp.exp2(m_prev - m_new)
            p_t = jnp.exp2(s_t - m_new)
            l_st[:, ls] = alpha * l_st[:, ls] + jnp.sum(p_t, axis=0, keepdims=True)
            acc_st[:, ls] = alpha * acc_st[:, ls] + jnp.dot(v_t, p_t.astype(BF16), preferred_element_type=F32)
            m_st[:, ls] = m_new

    sel_state = (m_ref, l_ref, acc_ref)
    win_state = (m2_ref, l2_ref, acc2_ref)
    aug = slice(0, 2 * HEAD_DIM)
    plain = slice(0, HEAD_DIM)
    causal = lambda k, q: k <= q
    row0 = pl.multiple_of(c * tq, tq)

    has_prev = c >= 1
    cp = jnp.maximum(c - 1, 0)
    rowp = pl.multiple_of(cp * tq, tq)
    scores_into(2, kw_ref[pl.ds(row0, tq), :], plain, causal)
    scores_into(3, kw_ref[pl.ds(rowp, tq), :], plain, lambda k, q: jnp.logical_and(k > q, has_prev))
    scores_into(0, kaug_ref[pl.ds(row0, tq), :], aug, causal)
    consume(2, vwt_ref[c], win_state)
    consume(3, vwt_ref[cp], win_state)

    def key_rows(k):
        return kaug_ref[pl.ds(pl.multiple_of((k - 1) * tq, tq), tq), :]

    def v_of(k):
        return vst_ref[jnp.where(k == 0, c, k - 1)]

    def sel_body(j, carry):
        k = 2 * j
        scores_into(1, key_rows(k + 1), aug, None)
        consume(0, v_of(k), sel_state)
        scores_into(0, key_rows(k + 2), aug, None)
        consume(1, v_of(k + 1), sel_state)
        return carry
    n_pairs = lax.shift_right_logical(c, 1)
    lax.fori_loop(0, n_pairs, sel_body, 0)
    odd = (c & 1) == 1

    @pl.when(odd)
    def _():
        scores_into(1, key_rows(c), aug, None)
        consume(0, v_of(c - 1), sel_state)
        consume(1, v_of(c), sel_state)

    @pl.when(jnp.logical_not(odd))
    def _():
        consume(0, v_of(c), sel_state)

    o_sel_t = acc_ref[...] / l_ref[...]
    o_win_t = acc2_ref[...] / l2_ref[...]

    gates = jax.nn.sigmoid(gate_ref[...])
    gates_t = jax.nn.sigmoid(gate_t_ref[...])
    for j in range(NSA_QPG):
        sl = slice(j * HEAD_DIM, (j + 1) * HEAD_DIM)
        ql = slice(j * tq, (j + 1) * tq)
        g_s = gates_t[N_NSA_GATES * j + 1:N_NSA_GATES * j + 2, :]
        g_w = gates_t[N_NSA_GATES * j + 2:N_NSA_GATES * j + 3, :]
        mixed_t = g_s * o_sel_t[:, ql] + g_w * o_win_t[:, ql]
        g_c = gates[:, N_NSA_GATES * j:N_NSA_GATES * j + 1]
        o_ref[:, sl] = (g_c * oc_ref[:, sl] + mixed_t.T).astype(o_ref.dtype)


def nsa_attention(q_rot, bias, kv4, o_c, gates):
    s = q_rot.shape[0]
    tq = min(NSA_TQ, s)
    assert WIN == tq
    qw = NSA_QPG * HEAD_DIM
    nq = NSA_QPG * tq
    kvspec = lambda off: pl.BlockSpec((s, HEAD_DIM), functools.partial(lambda g, c, o: (0, o + g), o=off))
    ng = NSA_QPG * N_NSA_GATES
    gates_t = gates.transpose(0, 2, 1)
    return pl.pallas_call(
        _nsa_attn_kernel, grid=(NSA_KV, s // tq),
        in_specs=[pl.BlockSpec((tq, qw), lambda g, c: (c, g)),
                  pl.BlockSpec((None, tq, SEL_LANES), lambda g, c: (g, c, 0)),
                  kvspec(0), kvspec(NSA_KV), kvspec(2 * NSA_KV), kvspec(3 * NSA_KV),
                  pl.BlockSpec((tq, qw), lambda g, c: (c, g)),
                  pl.BlockSpec((None, tq, ng), lambda g, c: (g, c, 0)),
                  pl.BlockSpec((None, ng, tq), lambda g, c: (g, 0, c))],
        out_specs=pl.BlockSpec((tq, qw), lambda g, c: (c, g)),
        out_shape=jax.ShapeDtypeStruct((s, B_W), BF16),
        scratch_shapes=[pltpu.VMEM((s, 2 * HEAD_DIM), BF16),
                        pltpu.VMEM((s // tq, HEAD_DIM, tq), BF16),
                        pltpu.VMEM((s // tq, HEAD_DIM, tq), BF16),
                        pltpu.VMEM((nq, 2 * HEAD_DIM), BF16),
                        pltpu.VMEM((4, tq, nq), F32),
                        pltpu.VMEM((1, nq), F32), pltpu.VMEM((1, nq), F32), pltpu.VMEM((HEAD_DIM, nq), F32),
                        pltpu.VMEM((1, nq), F32), pltpu.VMEM((1, nq), F32), pltpu.VMEM((HEAD_DIM, nq), F32)],
        compiler_params=_params("arbitrary", "arbitrary"), name="nsa_attention",
    )(q_rot, bias, kv4, kv4, kv4, kv4, o_c, gates, gates_t)


def _overlap_matrix(n_cmp_rows, n_sel):
    cs = jnp.arange(n_cmp_rows)[:, None] * CMP_STRIDE
    ss = jnp.arange(SEL_LANES)[None, :] * SEL_LEN
    ov = jnp.clip(jnp.minimum(cs + CMP_LEN, ss + SEL_LEN) - jnp.maximum(cs, ss), 0, None)
    ov = jnp.where(jnp.arange(SEL_LANES)[None, :] < n_sel, ov, 0)
    return (ov.astype(F32) / CMP_LEN).astype(BF16)


def _layer(x, p, g_mix_pre, w_in, pe_ck, w_ck1, w_ck2, pe_cv, w_cv1, w_cv2, w_a, w_b, w_out,
           g_mix_post, g_ffn_pre, w_gu, w_down, g_ffn_post, g_ple_pre, w_ple_gate, w_ple, g_ple_post):
    s, d = x.shape
    d_ff = w_down.shape[0]
    tm = min(1024, s)
    pos = jnp.arange(s)
    ij0 = lambda i, j: (i, 0)
    ijj = lambda i, j: (i, j)

    h = rmsnorm_bf16(x, g_mix_pre)
    w_in_t = w_in.T

    cos, sin = _rope_tables(pos)
    tab = [(cos, (tm, HEAD_DIM), ij0), (sin, (tm, HEAD_DIM), ij0)]
    tn = DIL_W
    blk = lambda off: off // tn

    qkv_a, = fused_matmul([h], [(w_in_t,blk(OFF_QA))], [0], 3 * A_W // tn, tm, tn,
                          _epi_rope_where(lambda j: j < 2 * A_W // tn), [(3 * A_W, F32, ijj)], tab, name="proj_dil",
                          rhs_t=True)
    groups = [dilated_group_attention(qkv_a, gi, rate) for gi, rate in enumerate(DIL_RATES)]
    attn_a = dilated_combine([o for o, _ in groups], [l for _, l in groups])

    q_raw, q_rot = fused_matmul([h], [(w_in_t,blk(OFF_QB))], [0], B_W // tn, tm, tn, _epi_dual_rope,
                                [(B_W, BF16, ijj), (B_W, BF16, ijj)], tab, name="proj_nsa_q", rhs_t=True)
    kvc, = fused_matmul([h], [(w_in_t,blk(OFF_KC))], [0], 2 * KV_W // tn, tm, tn, _epi_plain,
                        [(2 * KV_W, F32, ijj)], name="proj_nsa_cmp", rhs_t=True)
    kv4, = fused_matmul([h], [(w_in_t,blk(OFF_KS))], [0], 4 * KV_W // tn, tm, tn,
                        _epi_rope_where(lambda j: lax.rem(j, 2) == 0), [(4 * KV_W, BF16, ijj)], tab,
                        name="proj_nsa_kv", rhs_t=True)
    gn, = fused_matmul([h], [(w_in_t,OFF_GN // LANES)], [0], 1, tm, LANES, _epi_plain,
                       [(LANES, F32, ijj)], name="proj_nsa_gates", rhs_t=True)
    gates = gn[:, :NSA_HEADS * N_NSA_GATES].reshape(s, NSA_KV, NSA_QPG * N_NSA_GATES).transpose(1, 0, 2)

    k_c = compress(kvc, 0, pe_ck, w_ck1, w_ck2)
    v_c = compress(kvc, NSA_KV, pe_cv, w_cv1, w_cv2)
    n_sel = s // SEL_LEN
    ov = _overlap_matrix(s // CMP_STRIDE, n_sel)
    o_c, bias = nsa_select(q_raw, k_c, v_c, ov, min(SEL_TOP, n_sel))
    attn_b = nsa_attention(q_rot, bias, kv4, o_c, gates)

    mixed = merge_branches(h, attn_a, attn_b, w_in_t, w_a, w_b, tm, min(256, d))
    tn_o = min(512, d)
    mo, = fused_matmul([mixed], [(w_out, 0)], [0], d // tn_o, tm, tn_o, _epi_plain, [(d, F32, ijj)], name="out_proj")
    x1, h2 = resid_norm(x, mo, g_mix_post, g_ffn_pre)

    tn_f = 256
    act, = fused_matmul([h2], [(w_gu, 0), (w_gu, d_ff // tn_f)], [0, 0], d_ff // tn_f, min(2 * tm, s), tn_f,
                        _epi_swiglu, [(d_ff, BF16, ijj)], name="ffn_up", lhs_buffers=1)
    dn, = fused_matmul([act], [(w_down, 0)], [0], d // tn_f, tm, tn_f, _epi_plain, [(d, F32, ijj)], name="ffn_down",
                       lhs_buffers=1)
    x2, h3 = resid_norm(x1, dn, g_ffn_post, g_ple_pre)

    ple, = fused_matmul([p, h3], [(w_ple, 0), (w_ple_gate, 0)], [0, 1], d // tn_o, tm, tn_o, _epi_ple,
                        [(d, F32, ijj)], name="ple")
    return resid_only(x2, ple, g_ple_post)


def kernel(x, p, g_mix_pre, w_in, pe_ck, w_ck1, w_ck2, pe_cv, w_cv1, w_cv2, w_a, w_b, w_out, g_mix_post,
           g_ffn_pre, w_gu, w_down, g_ffn_post, g_ple_pre, w_ple_gate, w_ple, g_ple_post):
    b, s, d = x.shape
    assert b == 1, "kernel supports batch 1"
    xs = x.reshape(s, d)
    for i in range(w_in.shape[0]):
        xs = _layer(xs, p[i].reshape(s, -1), g_mix_pre[i], w_in[i], pe_ck[i], w_ck1[i], w_ck2[i], pe_cv[i],
                    w_cv1[i], w_cv2[i], w_a[i], w_b[i], w_out[i], g_mix_post[i], g_ffn_pre[i], w_gu[i],
                    w_down[i], g_ffn_post[i], g_ple_pre[i], w_ple_gate[i], w_ple[i], g_ple_post[i])
    return xs.reshape(b, s, d)
```

```python
import functools

import jax
import jax.numpy as jnp
from jax import lax
from jax.experimental import pallas as pl
from jax.experimental.pallas import tpu as pltpu

F32 = jnp.float32
BF16 = jnp.bfloat16

HEAD_DIM = 128
LANES = 128
ROPE_THETA = 10000.0
NORM_EPS = 1e-6
NEG_INF = -1e30
SCALE = HEAD_DIM ** -0.5
LOG2E = 1.4426950408889634

DIL_RATES = (1, 4, 16)
DIL_HEADS = 4
DIL_SPAN = 128
DIL_W = DIL_HEADS * HEAD_DIM
N_DIL = len(DIL_RATES)
A_W = N_DIL * DIL_W

NSA_HEADS = 16
NSA_KV = 4
NSA_QPG = NSA_HEADS // NSA_KV
B_W = NSA_HEADS * HEAD_DIM
KV_W = NSA_KV * HEAD_DIM
CMP_LEN = 32
CMP_STRIDE = 16
SEL_LEN = 64
SEL_TOP = 16
WIN = 512
FORCE_BONUS = 1000.0
N_NSA_GATES = 3

OFF_QA, OFF_KA, OFF_VA = 0, A_W, 2 * A_W
OFF_QB = 3 * A_W
OFF_KC = OFF_QB + B_W
OFF_VC = OFF_KC + KV_W
OFF_KS = OFF_VC + KV_W
OFF_GN = OFF_KS + 4 * KV_W
OFF_GM = OFF_GN + NSA_HEADS * N_NSA_GATES

VMEM_LIMIT_BYTES = 56 * 1024 * 1024


def _params(*sem):
    return pltpu.CompilerParams(dimension_semantics=sem, vmem_limit_bytes=VMEM_LIMIT_BYTES)


def _rms(v, g):
    return v * lax.rsqrt(jnp.mean(v * v, axis=-1, keepdims=True) + NORM_EPS) * g


def _norm_kernel(x_ref, g_ref, h_ref):
    h_ref[...] = _rms(x_ref[...], g_ref[...]).astype(h_ref.dtype)


def _resid_norm_kernel(x_ref, y_ref, gp_ref, gn_ref, xo_ref, h_ref):
    xn = x_ref[...] + _rms(y_ref[...], gp_ref[...])
    xo_ref[...] = xn
    h_ref[...] = _rms(xn, gn_ref[...]).astype(h_ref.dtype)


def _resid_kernel(x_ref, y_ref, gp_ref, xo_ref):
    xo_ref[...] = x_ref[...] + _rms(y_ref[...], gp_ref[...])


def _row_block(s):
    return min(256, s)


def rmsnorm_bf16(x, g):
    s, d = x.shape
    tm = _row_block(s)
    row = pl.BlockSpec((tm, d), lambda i: (i, 0))
    vec = pl.BlockSpec((1, d), lambda i: (0, 0))
    return pl.pallas_call(
        _norm_kernel, grid=(s // tm,), in_specs=[row, vec], out_specs=row,
        out_shape=jax.ShapeDtypeStruct((s, d), BF16), compiler_params=_params("parallel"),
        name="rmsnorm")(x, g.reshape(1, d))


def resid_norm(x, y, g_post, g_next):
    s, d = x.shape
    tm = _row_block(s)
    row = pl.BlockSpec((tm, d), lambda i: (i, 0))
    vec = pl.BlockSpec((1, d), lambda i: (0, 0))
    return pl.pallas_call(
        _resid_norm_kernel, grid=(s // tm,), in_specs=[row, row, vec, vec], out_specs=[row, row],
        out_shape=[jax.ShapeDtypeStruct((s, d), F32), jax.ShapeDtypeStruct((s, d), BF16)],
        compiler_params=_params("parallel"), name="resid_norm")(x, y, g_post.reshape(1, d), g_next.reshape(1, d))


def resid_only(x, y, g_post):
    s, d = x.shape
    tm = _row_block(s)
    row = pl.BlockSpec((tm, d), lambda i: (i, 0))
    vec = pl.BlockSpec((1, d), lambda i: (0, 0))
    return pl.pallas_call(
        _resid_kernel, grid=(s // tm,), in_specs=[row, row, vec], out_specs=row,
        out_shape=jax.ShapeDtypeStruct((s, d), F32), compiler_params=_params("parallel"),
        name="resid")(x, y, g_post.reshape(1, d))


_NN = (((1,), (0,)), ((), ()))
_NT = (((1,), (1,)), ((), ()))


def _mm_kernel(*refs, n_lhs, lhs_of_rhs, n_extra, epilogue, rhs_t):
    n_rhs = len(lhs_of_rhs)
    lhs_refs = refs[:n_lhs]
    rhs_refs = refs[n_lhs:n_lhs + n_rhs]
    extra_refs = refs[n_lhs + n_rhs:n_lhs + n_rhs + n_extra]
    out_refs = refs[n_lhs + n_rhs + n_extra:]
    lhs_vals = [r[...].astype(BF16) for r in lhs_refs]
    accs = [lax.dot_general(lhs_vals[lhs_of_rhs[k]], rhs_refs[k][...].astype(BF16), _NT if rhs_t else _NN,
                            preferred_element_type=F32)
            for k in range(n_rhs)]
    epilogue(accs, extra_refs, out_refs)


def fused_matmul(lhs, rhs, lhs_of_rhs, n_blocks, tm, tn, epilogue, outs, extras=(), name="mm", lhs_buffers=2,
                 rhs_t=False):
    m = lhs[0].shape[0]
    in_specs = [pl.BlockSpec((tm, a.shape[1]), lambda i, j: (i, 0), pipeline_mode=pl.Buffered(lhs_buffers)) for a in lhs]
    for arr, col in rhs:
        col_fn = col if callable(col) else functools.partial(lambda j, o: j + o, o=col)
        if rhs_t:
            in_specs.append(pl.BlockSpec((tn, arr.shape[1]), functools.partial(lambda i, j, f: (f(j), 0), f=col_fn)))
        else:
            in_specs.append(pl.BlockSpec((arr.shape[0], tn), functools.partial(lambda i, j, f: (0, f(j)), f=col_fn)))
    for arr, bs, im in extras:
        in_specs.append(pl.BlockSpec(bs, im))
    out_specs = [pl.BlockSpec((tm, tn), im) for _, _, im in outs]
    out_shape = [jax.ShapeDtypeStruct((m, w), dt) for w, dt, _ in outs]
    body = functools.partial(_mm_kernel, n_lhs=len(lhs), lhs_of_rhs=tuple(lhs_of_rhs),
                             n_extra=len(extras), epilogue=epilogue, rhs_t=rhs_t)
    res = pl.pallas_call(
        body, grid=(m // tm, n_blocks), in_specs=in_specs, out_specs=out_specs, out_shape=out_shape,
        compiler_params=_params("parallel", "arbitrary"), name=name,
    )(*lhs, *[a for a, _ in rhs], *[a for a, _, _ in extras])
    return res


def _rope_tile(acc, cos, sin):
    parts = []
    for h in range(acc.shape[1] // HEAD_DIM):
        a = acc[:, h * HEAD_DIM:(h + 1) * HEAD_DIM]
        parts.append(a * cos + pltpu.roll(a, HEAD_DIM // 2, axis=1) * sin)
    return jnp.concatenate(parts, axis=1) if len(parts) > 1 else parts[0]


def _epi_plain(accs, extra, outs):
    outs[0][...] = accs[0].astype(outs[0].dtype)


def _epi_rope_where(rope_pred):
    def epi(accs, extra, outs):
        j = pl.program_id(1)
        cos_ref, sin_ref = extra

        @pl.when(rope_pred(j))
        def _():
            outs[0][...] = _rope_tile(accs[0], cos_ref[...], sin_ref[...]).astype(outs[0].dtype)

        @pl.when(jnp.logical_not(rope_pred(j)))
        def _():
            outs[0][...] = accs[0].astype(outs[0].dtype)
    return epi


def _epi_dual_rope(accs, extra, outs):
    cos_ref, sin_ref = extra
    outs[0][...] = accs[0].astype(outs[0].dtype)
    outs[1][...] = (_rope_tile(accs[0], cos_ref[...], sin_ref[...]) * (SCALE * LOG2E)).astype(outs[1].dtype)


def _merge_kernel(h_ref, a_ref, b_ref, wga_ref, wgb_ref, wa_ref, wb_ref, o_ref, pa_ref, pb_ref, *, shift):
    j = pl.program_id(1)
    tn = o_ref.shape[1]
    h = h_ref[...]
    pa_ref[:, tn:] = lax.dot_general(h, wga_ref[...].astype(BF16), _NT, preferred_element_type=F32)
    pb_ref[:, tn:] = lax.dot_general(h, wgb_ref[...].astype(BF16), _NT, preferred_element_type=F32)

    @pl.when(j > 0)
    def _():
        ya = jnp.dot(a_ref[...], wa_ref[...].astype(BF16), preferred_element_type=F32)
        yb = jnp.dot(b_ref[...], wb_ref[...].astype(BF16), preferred_element_type=F32)
        gate_a = jax.nn.sigmoid(pa_ref[:, shift:shift + tn])
        gate_b = jax.nn.sigmoid(pb_ref[:, shift:shift + tn])
        o_ref[...] = (gate_a * ya + gate_b * yb).astype(o_ref.dtype)

    pa_ref[:, :tn] = pa_ref[:, tn:]
    pb_ref[:, :tn] = pb_ref[:, tn:]


def merge_branches(h, attn_a, attn_b, w_in_t, w_a, w_b, tm, tn):
    m, d = h.shape
    nb = d // tn
    base, shift = OFF_GM // tn, OFF_GM % tn
    assert shift > 0 and d % tn == 0
    prev = lambda j: jnp.maximum(j - 1, 0)
    resident = lambda a: pl.BlockSpec((tm, a.shape[1]), lambda i, j: (i, 0), pipeline_mode=pl.Buffered(1))
    return pl.pallas_call(
        functools.partial(_merge_kernel, shift=shift), grid=(m // tm, nb + 1),
        in_specs=[resident(h), resident(attn_a), resident(attn_b),
                  pl.BlockSpec((tn, d), lambda i, j: (base + j, 0)),
                  pl.BlockSpec((tn, d), lambda i, j: (base + nb + j, 0)),
                  pl.BlockSpec((w_a.shape[0], tn), lambda i, j: (0, prev(j))),
                  pl.BlockSpec((w_b.shape[0], tn), lambda i, j: (0, prev(j)))],
        out_specs=pl.BlockSpec((tm, tn), lambda i, j: (i, prev(j))),
        out_shape=jax.ShapeDtypeStruct((m, d), BF16),
        scratch_shapes=[pltpu.VMEM((tm, 2 * tn), F32), pltpu.VMEM((tm, 2 * tn), F32)],
        compiler_params=_params("parallel", "arbitrary"), name="merge",
    )(h, attn_a, attn_b, w_in_t, w_in_t, w_a, w_b)


def _epi_swiglu(accs, extra, outs):
    gt, up = accs
    outs[0][...] = (gt * jax.nn.sigmoid(gt) * up).astype(outs[0].dtype)


def _epi_ple(accs, extra, outs):
    pw, gl = accs
    outs[0][...] = (pw * jax.nn.sigmoid(gl)).astype(outs[0].dtype)


def _rope_tables(pos):
    half = HEAD_DIM // 2
    inv = ROPE_THETA ** (-jnp.arange(half, dtype=F32) / half)
    ang = pos.astype(F32)[:, None] * inv[None, :]
    cos, sin = jnp.cos(ang), jnp.sin(ang)
    return jnp.concatenate([cos, cos], axis=1), jnp.concatenate([-sin, sin], axis=1)


DIL_TQ = DIL_SPAN


def _dil_kernel(q_ref, kp_ref, kc_ref, vp_ref, vc_ref, o_ref, lse_ref, *, rate):
    prev_ok = pl.program_id(1) > 0
    tq = DIL_TQ
    n_sub = q_ref.shape[0] // (tq * rate)
    row = lax.broadcasted_iota(jnp.int32, (tq, 2 * tq), 0)
    col = lax.broadcasted_iota(jnp.int32, (tq, 2 * tq), 1)
    band = (col >= row) & (col <= row + DIL_SPAN)
    band_first = band & ((col >= tq) | prev_ok)

    def tile(a, i):
        start = a + rate * tq * i
        return pl.ds(start, tq, stride=rate) if rate > 1 else pl.ds(start, tq)

    for a in range(rate):
        for i in range(n_sub):
            cur = tile(a, i)
            if i == 0:
                kp, vp, valid = kp_ref[tile(a, n_sub - 1), :], vp_ref[tile(a, n_sub - 1), :], band_first
            else:
                kp, vp, valid = kc_ref[tile(a, i - 1), :], vc_ref[tile(a, i - 1), :], band
            q = q_ref[cur, :].astype(BF16)
            k = jnp.concatenate([kp, kc_ref[cur, :]], axis=0).astype(BF16)
            v = jnp.concatenate([vp, vc_ref[cur, :]], axis=0).astype(BF16)
            s = lax.dot_general(q, k, (((1,), (1,)), ((), ())), preferred_element_type=F32) * SCALE
            s = jnp.where(valid, s, NEG_INF)
            m = jnp.max(s, axis=-1, keepdims=True)
            e = jnp.exp(s - m)
            den = jnp.sum(e, axis=-1, keepdims=True)
            o_ref[cur, :] = jnp.dot((e / den).astype(BF16), v, preferred_element_type=F32)
            lse_ref[cur, :] = jnp.broadcast_to(m + jnp.log(den), (tq, HEAD_DIM))


def dilated_group_attention(qkv, group, rate):
    s = qkv.shape[0]
    rows = min(s, DIL_TQ * max(DIL_RATES))
    heads_total = A_W // HEAD_DIM
    col = lambda part: functools.partial(lambda h, t, p: p * heads_total + group * DIL_HEADS + h, p=part)
    cur = lambda part: pl.BlockSpec((rows, HEAD_DIM), functools.partial(lambda h, t, f: (t, f(h, t)), f=col(part)))
    prv = lambda part: pl.BlockSpec((rows, HEAD_DIM),
                                    functools.partial(lambda h, t, f: (jnp.maximum(t - 1, 0), f(h, t)), f=col(part)))
    out = pl.BlockSpec((rows, HEAD_DIM), lambda h, t: (t, h))
    return pl.pallas_call(
        functools.partial(_dil_kernel, rate=rate),
        grid=(DIL_HEADS, s // rows), in_specs=[cur(0), prv(1), cur(1), prv(2), cur(2)], out_specs=[out, out],
        out_shape=[jax.ShapeDtypeStruct((s, DIL_W), F32)] * 2,
        compiler_params=_params("parallel", "parallel"), name=f"dilated_attn_r{rate}",
    )(qkv, qkv, qkv, qkv, qkv)


def _dil_combine_kernel(o0, o1, o2, l0, l1, l2, out_ref):
    la, lb, lc = l0[...], l1[...], l2[...]
    mx = jnp.maximum(jnp.maximum(la, lb), lc)
    wa, wb, wc = jnp.exp(la - mx), jnp.exp(lb - mx), jnp.exp(lc - mx)
    tot = wa + wb + wc
    out_ref[...] = ((wa / tot) * o0[...] + (wb / tot) * o1[...] + (wc / tot) * o2[...]).astype(out_ref.dtype)


def dilated_combine(os_, ls_):
    s = os_[0].shape[0]
    tm = min(512, s)
    blk = pl.BlockSpec((tm, DIL_W), lambda i: (i, 0))
    return pl.pallas_call(
        _dil_combine_kernel, grid=(s // tm,), in_specs=[blk] * 6, out_specs=blk,
        out_shape=jax.ShapeDtypeStruct((s, DIL_W), BF16), compiler_params=_params("parallel"),
        name="dilated_combine")(*os_, *ls_)


def _gelu_tanh(x):
    return 0.5 * x * (1.0 + jnp.tanh(0.7978845608028654 * (x + 0.044715 * (x * x * x))))


def _compress_kernel(kv_ref, pe_ref, w1_ref, w2_ref, o_ref):
    n = kv_ref.shape[0] // CMP_STRIDE
    h_lo = jnp.zeros((n, w1_ref.shape[1]), F32)
    h_hi = jnp.zeros((n, w1_ref.shape[1]), F32)
    for l in range(CMP_STRIDE):
        x = kv_ref[pl.ds(l, n, stride=CMP_STRIDE), :]
        lo, hi = l, CMP_STRIDE + l
        h_lo = h_lo + jnp.dot((x + pe_ref[lo:lo + 1, :]).astype(BF16),
                              w1_ref[lo * HEAD_DIM:(lo + 1) * HEAD_DIM, :].astype(BF16), preferred_element_type=F32)
        h_hi = h_hi + jnp.dot((x + pe_ref[hi:hi + 1, :]).astype(BF16),
                              w1_ref[hi * HEAD_DIM:(hi + 1) * HEAD_DIM, :].astype(BF16), preferred_element_type=F32)
    hid = h_lo + pltpu.roll(h_hi, n - 1, axis=0)
    act = _gelu_tanh(hid).astype(BF16)
    o_ref[...] = jnp.dot(act, w2_ref[...].astype(BF16), preferred_element_type=F32).astype(o_ref.dtype)


def compress(kvc, first_col_block, pe, w1, w2):
    s = kvc.shape[0]
    n = s // CMP_STRIDE
    return pl.pallas_call(
        _compress_kernel, grid=(NSA_KV,),
        in_specs=[pl.BlockSpec((s, HEAD_DIM), lambda g: (0, first_col_block + g)),
                  pl.BlockSpec(pe.shape, lambda g: (0, 0)),
                  pl.BlockSpec(w1.shape, lambda g: (0, 0)),
                  pl.BlockSpec(w2.shape, lambda g: (0, 0))],
        out_specs=pl.BlockSpec((None, n, HEAD_DIM), lambda g: (g, 0, 0)),
        out_shape=jax.ShapeDtypeStruct((NSA_KV, n, HEAD_DIM), BF16),
        compiler_params=_params("parallel"), name="nsa_compress")(kvc, pe, w1, w2)


NSA_TQ = 512
NSA_CHAINS = 4
SEL_LANES = 128


def _nsa_select_kernel(q_ref, kc_ref, vc_ref, ov_ref, oc_ref, bias_ref, *, n_top):
    c = pl.program_id(1)
    tq = q_ref.shape[0]
    nck = kc_ref.shape[0]
    pos = c * tq + lax.broadcasted_iota(jnp.int32, (tq, 1), 0)
    n_idx = lax.broadcasted_iota(jnp.int32, (1, nck), 1)
    cvalid = (n_idx * CMP_STRIDE + (CMP_LEN - 1)) <= pos
    has_any = (pos >= CMP_LEN - 1).astype(F32)
    kc, vc, ov = kc_ref[...], vc_ref[...], ov_ref[...]
    imp = jnp.zeros((tq, SEL_LANES), F32)
    for j in range(NSA_QPG):
        sl = slice(j * HEAD_DIM, (j + 1) * HEAD_DIM)
        s = lax.dot_general(q_ref[:, sl], kc, (((1,), (1,)), ((), ())), preferred_element_type=F32) * SCALE
        s = jnp.where(cvalid, s, NEG_INF)
        m = jnp.max(s, axis=-1, keepdims=True)
        e = jnp.exp(s - m)
        p = ((e / jnp.sum(e, axis=-1, keepdims=True)) * has_any).astype(BF16)
        oc_ref[:, sl] = jnp.dot(p, vc, preferred_element_type=F32)
        imp = imp + jnp.dot(p, ov, preferred_element_type=F32)
    blk = lax.broadcasted_iota(jnp.int32, (1, SEL_LANES), 1)
    cur = lax.shift_right_logical(pos, SEL_LEN.bit_length() - 1)
    forced = ((blk == 0) | (blk == cur) | (blk == cur - 1)).astype(F32)
    svalid = blk * SEL_LEN <= pos
    work = jnp.where(svalid, imp + FORCE_BONUS * forced, NEG_INF)
    work = work.T
    blk_f = lax.broadcasted_iota(jnp.int32, (SEL_LANES, tq), 0).astype(F32)
    chosen = jnp.zeros((SEL_LANES, tq), F32)
    for _ in range(n_top):
        mx = jnp.max(work, axis=0, keepdims=True)
        first = jnp.min(jnp.where(work == mx, blk_f, float(SEL_LANES)), axis=0, keepdims=True)
        hit = blk_f == first
        chosen = jnp.where(hit, 1.0, chosen)
        work = jnp.where(hit, -jnp.inf, work)
    keep = (chosen.T > 0.0) & svalid
    bias_ref[...] = jnp.where(keep, 0.0, NEG_INF / SCALE).astype(bias_ref.dtype)


def nsa_select(q_raw, k_c, v_c, ov, n_top):
    s = q_raw.shape[0]
    tq = min(NSA_TQ, s)
    nck = k_c.shape[1]
    return pl.pallas_call(
        functools.partial(_nsa_select_kernel, n_top=n_top), grid=(NSA_KV, s // tq),
        in_specs=[pl.BlockSpec((tq, NSA_QPG * HEAD_DIM), lambda g, c: (c, g)),
                  pl.BlockSpec((None, nck, HEAD_DIM), lambda g, c: (g, 0, 0)),
                  pl.BlockSpec((None, nck, HEAD_DIM), lambda g, c: (g, 0, 0)),
                  pl.BlockSpec((nck, SEL_LANES), lambda g, c: (0, 0))],
        out_specs=[pl.BlockSpec((tq, NSA_QPG * HEAD_DIM), lambda g, c: (c, g)),
                   pl.BlockSpec((None, tq, SEL_LANES), lambda g, c: (g, c, 0))],
        out_shape=[jax.ShapeDtypeStruct((s, B_W), F32),
                   jax.ShapeDtypeStruct((NSA_KV, s, SEL_LANES), BF16)],
        compiler_params=_params("parallel", "parallel"), name="nsa_select")(q_raw, k_c, v_c, ov)


def _nsa_attn_kernel(qr_ref, bias_ref, ks_ref, vs_ref, kw_ref, vw_ref, oc_ref, gate_ref, gate_t_ref, o_ref,
                     kaug_ref, vst_ref, vwt_ref, qaug_ref, sbuf_ref, m_ref, l_ref, acc_ref, m2_ref, l2_ref, acc2_ref):
    c = pl.program_id(1)
    tq = qr_ref.shape[0]
    nq = NSA_QPG * tq
    n_tiles = ks_ref.shape[0] // tq

    @pl.when(c == 0)
    def _build_keys():
        def body(i, carry):
            r0 = pl.multiple_of(i * tq, tq)
            kaug_ref[pl.ds(r0, tq), 0:HEAD_DIM] = ks_ref[pl.ds(r0, tq), :]
            key = r0 + lax.broadcasted_iota(jnp.int32, (tq, SEL_LANES), 0)
            lane = lax.broadcasted_iota(jnp.int32, (tq, SEL_LANES), 1)
            onehot = lax.shift_right_logical(key, SEL_LEN.bit_length() - 1) == lane
            kaug_ref[pl.ds(r0, tq), HEAD_DIM:2 * HEAD_DIM] = jnp.where(onehot, 1.0, 0.0).astype(BF16)
            vst_ref[i] = vs_ref[pl.ds(r0, tq), :].astype(F32).T.astype(BF16)
            vwt_ref[i] = vw_ref[pl.ds(r0, tq), :].astype(F32).T.astype(BF16)
            return carry
        lax.fori_loop(0, n_tiles, body, 0)

    for j in range(NSA_QPG):
        qaug_ref[j * tq:(j + 1) * tq, 0:HEAD_DIM] = qr_ref[:, j * HEAD_DIM:(j + 1) * HEAD_DIM]
        qaug_ref[j * tq:(j + 1) * tq, HEAD_DIM:2 * HEAD_DIM] = bias_ref[...]

    nl = nq // NSA_CHAINS
    lanes = [slice(h * nl, (h + 1) * nl) for h in range(NSA_CHAINS)]
    nt = (((1,), (1,)), ((), ()))
    for st in (m_ref, m2_ref):
        st[...] = jnp.full(st.shape, NEG_INF, F32)
    for st in (l_ref, l2_ref, acc_ref, acc2_ref):
        st[...] = jnp.zeros(st.shape, F32)

    def scores_into(slot, k_rows, q_cols, mask_fn):
        for ls in lanes:
            s_t = lax.dot_general(k_rows, qaug_ref[ls, q_cols], nt, preferred_element_type=F32)
            if mask_fn is not None:
                key_i = lax.broadcasted_iota(jnp.int32, (tq, nl), 0)
                qry_i = lax.broadcasted_iota(jnp.int32, (tq, nl), 1) & (tq - 1)
                s_t = jnp.where(mask_fn(key_i, qry_i), s_t, NEG_INF)
            sbuf_ref[slot, :, ls] = s_t

    def consume(slot, v_t, state):
        m_st, l_st, acc_st = state
        for ls in lanes:
            s_t = sbuf_ref[slot, :, ls]
            m_prev = m_st[:, ls]
            m_new = jnp.maximum(m_prev, jnp.max(s_t, axis=0, keepdims=True))
            alpha = jnp.exp2(m_prev - m_new)
            p_t = jnp.exp2(s_t - m_new)
            l_st[:, ls] = alpha * l_st[:, ls] + jnp.sum(p_t, axis=0, keepdims=True)
            acc_st[:, ls] = alpha * acc_st[:, ls] + jnp.dot(v_t, p_t.astype(BF16), preferred_element_type=F32)
            m_st[:, ls] = m_new

    sel_state = (m_ref, l_ref, acc_ref)
    win_state = (m2_ref, l2_ref, acc2_ref)
    aug = slice(0, 2 * HEAD_DIM)
    plain = slice(0, HEAD_DIM)
    causal = lambda k, q: k <= q
    row0 = pl.multiple_of(c * tq, tq)

    has_prev = c >= 1
    cp = jnp.maximum(c - 1, 0)
    rowp = pl.multiple_of(cp * tq, tq)
    scores_into(2, kw_ref[pl.ds(row0, tq), :], plain, causal)
    scores_into(3, kw_ref[pl.ds(rowp, tq), :], plain, lambda k, q: jnp.logical_and(k > q, has_prev))
    scores_into(0, kaug_ref[pl.ds(row0, tq), :], aug, causal)
    consume(2, vwt_ref[c], win_state)
    consume(3, vwt_ref[cp], win_state)

    def key_rows(k):
        return kaug_ref[pl.ds(pl.multiple_of((k - 1) * tq, tq), tq), :]

    def v_of(k):
        return vst_ref[jnp.where(k == 0, c, k - 1)]

    def sel_body(j, carry):
        k = 2 * j
        scores_into(1, key_rows(k + 1), aug, None)
        consume(0, v_of(k), sel_state)
        scores_into(0, key_rows(k + 2), aug, None)
        consume(1, v_of(k + 1), sel_state)
        return carry
    n_pairs = lax.shift_right_logical(c, 1)
    lax.fori_loop(0, n_pairs, sel_body, 0)
    odd = (c & 1) == 1

    @pl.when(odd)
    def _():
        scores_into(1, key_rows(c), aug, None)
        consume(0, v_of(c - 1), sel_state)
        consume(1, v_of(c), sel_state)

    @pl.when(jnp.logical_not(odd))
    def _():
        consume(0, v_of(c), sel_state)

    o_sel_t = acc_ref[...] / l_ref[...]
    o_win_t = acc2_ref[...] / l2_ref[...]

    gates = jax.nn.sigmoid(gate_ref[...])
    gates_t = jax.nn.sigmoid(gate_t_ref[...])
    for j in range(NSA_QPG):
        sl = slice(j * HEAD_DIM, (j + 1) * HEAD_DIM)
        ql = slice(j * tq, (j + 1) * tq)
        g_s = gates_t[N_NSA_GATES * j + 1:N_NSA_GATES * j + 2, :]
        g_w = gates_t[N_NSA_GATES * j + 2:N_NSA_GATES * j + 3, :]
        mixed_t = g_s * o_sel_t[:, ql] + g_w * o_win_t[:, ql]
        g_c = gates[:, N_NSA_GATES * j:N_NSA_GATES * j + 1]
        o_ref[:, sl] = (g_c * oc_ref[:, sl] + mixed_t.T).astype(o_ref.dtype)


def nsa_attention(q_rot, bias, kv4, o_c, gates):
    s = q_rot.shape[0]
    tq = min(NSA_TQ, s)
    assert WIN == tq
    qw = NSA_QPG * HEAD_DIM
    nq = NSA_QPG * tq
    kvspec = lambda off: pl.BlockSpec((s, HEAD_DIM), functools.partial(lambda g, c, o: (0, o + g), o=off))
    ng = NSA_QPG * N_NSA_GATES
    gates_t = gates.transpose(0, 2, 1)
    return pl.pallas_call(
        _nsa_attn_kernel, grid=(NSA_KV, s // tq),
        in_specs=[pl.BlockSpec((tq, qw), lambda g, c: (c, g)),
                  pl.BlockSpec((None, tq, SEL_LANES), lambda g, c: (g, c, 0)),
                  kvspec(0), kvspec(NSA_KV), kvspec(2 * NSA_KV), kvspec(3 * NSA_KV),
                  pl.BlockSpec((tq, qw), lambda g, c: (c, g)),
                  pl.BlockSpec((None, tq, ng), lambda g, c: (g, c, 0)),
                  pl.BlockSpec((None, ng, tq), lambda g, c: (g, 0, c))],
        out_specs=pl.BlockSpec((tq, qw), lambda g, c: (c, g)),
        out_shape=jax.ShapeDtypeStruct((s, B_W), BF16),
        scratch_shapes=[pltpu.VMEM((s, 2 * HEAD_DIM), BF16),
                        pltpu.VMEM((s // tq, HEAD_DIM, tq), BF16),
                        pltpu.VMEM((s // tq, HEAD_DIM, tq), BF16),
                        pltpu.VMEM((nq, 2 * HEAD_DIM), BF16),
                        pltpu.VMEM((4, tq, nq), F32),
                        pltpu.VMEM((1, nq), F32), pltpu.VMEM((1, nq), F32), pltpu.VMEM((HEAD_DIM, nq), F32),
                        pltpu.VMEM((1, nq), F32), pltpu.VMEM((1, nq), F32), pltpu.VMEM((HEAD_DIM, nq), F32)],
        compiler_params=_params("arbitrary", "arbitrary"), name="nsa_attention",
    )(q_rot, bias, kv4, kv4, kv4, kv4, o_c, gates, gates_t)


def _overlap_matrix(n_cmp_rows, n_sel):
    cs = jnp.arange(n_cmp_rows)[:, None] * CMP_STRIDE
    ss = jnp.arange(SEL_LANES)[None, :] * SEL_LEN
    ov = jnp.clip(jnp.minimum(cs + CMP_LEN, ss + SEL_LEN) - jnp.maximum(cs, ss), 0, None)
    ov = jnp.where(jnp.arange(SEL_LANES)[None, :] < n_sel, ov, 0)
    return (ov.astype(F32) / CMP_LEN).astype(BF16)


def _layer(x, p, g_mix_pre, w_in, pe_ck, w_ck1, w_ck2, pe_cv, w_cv1, w_cv2, w_a, w_b, w_out,
           g_mix_post, g_ffn_pre, w_gu, w_down, g_ffn_post, g_ple_pre, w_ple_gate, w_ple, g_ple_post):
    s, d = x.shape
    d_ff = w_down.shape[0]
    tm = min(1024, s)
    pos = jnp.arange(s)
    ij0 = lambda i, j: (i, 0)
    ijj = lambda i, j: (i, j)

    h = rmsnorm_bf16(x, g_mix_pre)
    w_in_t = w_in.T

    cos, sin = _rope_tables(pos)
    tab = [(cos, (tm, HEAD_DIM), ij0), (sin, (tm, HEAD_DIM), ij0)]
    tn = DIL_W
    blk = lambda off: off // tn

    qkv_a, = fused_matmul([h], [(w_in_t,blk(OFF_QA))], [0], 3 * A_W // tn, tm, tn,
                          _epi_rope_where(lambda j: j < 2 * A_W // tn), [(3 * A_W, F32, ijj)], tab, name="proj_dil",
                          rhs_t=True)
    groups = [dilated_group_attention(qkv_a, gi, rate) for gi, rate in enumerate(DIL_RATES)]
    attn_a = dilated_combine([o for o, _ in groups], [l for _, l in groups])

    q_raw, q_rot = fused_matmul([h], [(w_in_t,blk(OFF_QB))], [0], B_W // tn, tm, tn, _epi_dual_rope,
                                [(B_W, BF16, ijj), (B_W, BF16, ijj)], tab, name="proj_nsa_q", rhs_t=True)
    kvc, = fused_matmul([h], [(w_in_t,blk(OFF_KC))], [0], 2 * KV_W // tn, tm, tn, _epi_plain,
                        [(2 * KV_W, F32, ijj)], name="proj_nsa_cmp", rhs_t=True)
    kv4, = fused_matmul([h], [(w_in_t,blk(OFF_KS))], [0], 4 * KV_W // tn, tm, tn,
                        _epi_rope_where(lambda j: lax.rem(j, 2) == 0), [(4 * KV_W, BF16, ijj)], tab,
                        name="proj_nsa_kv", rhs_t=True)
    gn, = fused_matmul([h], [(w_in_t,OFF_GN // LANES)], [0], 1, tm, LANES, _epi_plain,
                       [(LANES, F32, ijj)], name="proj_nsa_gates", rhs_t=True)
    gates = gn[:, :NSA_HEADS * N_NSA_GATES].reshape(s, NSA_KV, NSA_QPG * N_NSA_GATES).transpose(1, 0, 2)

    k_c = compress(kvc, 0, pe_ck, w_ck1, w_ck2)
    v_c = compress(kvc, NSA_KV, pe_cv, w_cv1, w_cv2)
    n_sel = s // SEL_LEN
    ov = _overlap_matrix(s // CMP_STRIDE, n_sel)
    o_c, bias = nsa_select(q_raw, k_c, v_c, ov, min(SEL_TOP, n_sel))
    attn_b = nsa_attention(q_rot, bias, kv4, o_c, gates)

    mixed = merge_branches(h, attn_a, attn_b, w_in_t, w_a, w_b, tm, min(256, d))
    tn_o = min(512, d)
    mo, = fused_matmul([mixed], [(w_out, 0)], [0], d // tn_o, tm, tn_o, _epi_plain, [(d, F32, ijj)], name="out_proj")
    x1, h2 = resid_norm(x, mo, g_mix_post, g_ffn_pre)

    tn_f = 256
    act, = fused_matmul([h2], [(w_gu, 0), (w_gu, d_ff // tn_f)], [0, 0], d_ff // tn_f, min(2 * tm, s), tn_f,
                        _epi_swiglu, [(d_ff, BF16, ijj)], name="ffn_up", lhs_buffers=1)
    dn, = fused_matmul([act], [(w_down, 0)], [0], d // tn_f, tm, tn_f, _epi_plain, [(d, F32, ijj)], name="ffn_down",
                       lhs_buffers=1)
    x2, h3 = resid_norm(x1, dn, g_ffn_post, g_ple_pre)

    ple, = fused_matmul([p, h3], [(w_ple, 0), (w_ple_gate, 0)], [0, 1], d // tn_o, tm, tn_o, _epi_ple,
                        [(d, F32, ijj)], name="ple")
    return resid_only(x2, ple, g_ple_post)


def kernel(x, p, g_mix_pre, w_in, pe_ck, w_ck1, w_ck2, pe_cv, w_cv1, w_cv2, w_a, w_b, w_out, g_mix_post,
           g_ffn_pre, w_gu, w_down, g_ffn_post, g_ple_pre, w_ple_gate, w_ple, g_ple_post):
    b, s, d = x.shape
    assert b == 1, "kernel supports batch 1"
    xs = x.reshape(s, d)
    for i in range(w_in.shape[0]):
        xs = _layer(xs, p[i].reshape(s, -1), g_mix_pre[i], w_in[i], pe_ck[i], w_ck1[i], w_ck2[i], pe_cv[i],
                    w_cv1[i], w_cv2[i], w_a[i], w_b[i], w_out[i], g_mix_post[i], g_ffn_pre[i], w_gu[i],
                    w_down[i], g_ffn_post[i], g_ple_pre[i], w_ple_gate[i], w_ple[i], g_ple_post[i])
    return xs.reshape(b, s, d)
```
